```python
import math
import jax, jax.numpy as jnp
from jax import lax
import numpy as np

D_MODEL = 1024
BATCH = 8
SEQ = 2048
DEPTH = 1
DEC_BATCH = 128
DEC_SEQ = 4
PAST_LEN = 16384
PAGE_SIZE = 128

MIX_WIDTH = D_MODEL
A_HEADS = 4
A_DK = 128
A_DV = 128
A_WIDTH = A_HEADS * A_DK
HGRN_CHUNK = 64
B_WIDTH = MIX_WIDTH - A_WIDTH
B_GROUPS = 4
B_CH = B_WIDTH // B_GROUPS
GMLP_CHUNK = 128
IN_COLS = 4 * A_WIDTH + 2 * B_WIDTH
MEM_LEN = 256
X_HEADS = 4
X_HD = D_MODEL // X_HEADS
N_EXPERTS = 32
TOP_K = 4
D_FF = D_MODEL
SWIGLU_LIMIT = 7.0
SWIGLU_ALPHA = 1.702
MOE_BLOCK = 128
EPS = 1e-6

kernel_name = "hymba_hgrn2_gmlp_xmem_moe_step"


def rms_norm(x, g):
    xf = x.astype(jnp.float32)
    y = xf * lax.rsqrt(jnp.mean(xf * xf, axis=-1, keepdims=True) + EPS)
    return (y * g.astype(jnp.float32)).astype(x.dtype)


def hgrn2_recurrence(q, k, v, logf, S0):
    f32 = jnp.float32
    Bn, L, H, _ = q.shape
    DV = v.shape[-1]
    c = math.gcd(L, HGRN_CHUNK)
    n = L // c

    def chunks(t):
        return t.astype(f32).reshape(Bn, n, c, H, t.shape[-1]).transpose(1, 0, 3, 2, 4)

    causal = jnp.tril(jnp.ones((c, c), dtype=bool))[:, :, None]

    def step(S, inp):
        qc, kc, vc, gc = inp
        b = jnp.cumsum(gc, axis=2)
        o_inter = jnp.einsum("bhtk,bhkv->bhtv", qc * jnp.exp(b), S)
        rel = b[:, :, :, None, :] - b[:, :, None, :, :]
        decay = jnp.exp(jnp.where(causal, rel, -jnp.inf))
        scores = jnp.einsum("bhtk,bhsk,bhtsk->bhts", qc, kc, decay)
        o_intra = jnp.einsum("bhts,bhsv->bhtv", scores, vc)
        b_end = b[:, :, -1:, :]
        S_new = (jnp.exp(b_end[:, :, 0, :])[..., None] * S
                 + jnp.einsum("bhsk,bhsv->bhkv", kc * jnp.exp(b_end - b), vc))
        return S_new, o_inter + o_intra

    S_fin, o = lax.scan(step, S0.astype(f32), (chunks(q), chunks(k), chunks(v), chunks(logf)))
    o = o.transpose(1, 0, 3, 2, 4).reshape(Bn, L, H, DV)
    return o, S_fin


def gmlp_spatial(u, vn, w_s, b_s):
    Bn, L, G, C = u.shape
    c = min(L, GMLP_CHUNK)
    n = L // c
    w = jnp.tril(w_s[:, :c, :c])
    vc = vn.reshape(Bn, n, c, G, C)
    s = jnp.einsum("gts,bnsgc->bntgc", w, vc) + b_s[:, :c].T[None, None, :, :, None]
    return u * s.reshape(Bn, L, G, C).astype(u.dtype)


def token_mixer(h, S0, lb, w_in, w_out, hgrn_norm_g, gmlp_ln_g, gmlp_ln_b, gmlp_w_s, gmlp_b_s):
    f32 = jnp.float32
    Bn, L, _ = h.shape
    p = h @ w_in
    q, f, i, g, u, v = jnp.split(
        p, [A_WIDTH, 2 * A_WIDTH, 3 * A_WIDTH, 4 * A_WIDTH, 4 * A_WIDTH + B_WIDTH], axis=-1)
    heads = lambda t: t.reshape(Bn, L, A_HEADS, -1)
    fg = lb + (1.0 - lb) * jax.nn.sigmoid(f.astype(f32))
    o, S_fin = hgrn2_recurrence(heads(jax.nn.silu(q)), heads(1.0 - fg), heads(i), heads(jnp.log(fg)), S0)
    o = o * lax.rsqrt(jnp.mean(o * o, axis=-1, keepdims=True) + EPS)
    o = o * hgrn_norm_g.reshape(A_HEADS, A_DV).astype(f32)
    o_a = (o.reshape(Bn, L, A_WIDTH) * jax.nn.silu(g.astype(f32))).astype(h.dtype)
    ug = jax.nn.gelu(u).reshape(Bn, L, B_GROUPS, B_CH)
    vf = jax.nn.gelu(v).reshape(Bn, L, B_GROUPS, B_CH).astype(f32)
    mu = jnp.mean(vf, axis=-1, keepdims=True)
    var = jnp.mean(jnp.square(vf - mu), axis=-1, keepdims=True)
    vn = ((vf - mu) * lax.rsqrt(var + EPS) * gmlp_ln_g.astype(f32) + gmlp_ln_b.astype(f32)).astype(h.dtype)
    o_b = gmlp_spatial(ug, vn, gmlp_w_s, gmlp_b_s).reshape(Bn, L, B_WIDTH)
    y = jnp.concatenate([o_a, o_b], axis=-1) @ w_out
    return y, S_fin, vn


def memory_kv(mem, norm_mem_g, w_xk, w_xv):
    Bn, M, _ = mem.shape
    mn = rms_norm(mem, norm_mem_g)
    return ((mn @ w_xk).reshape(Bn, M, X_HEADS, X_HD), (mn @ w_xv).reshape(Bn, M, X_HEADS, X_HD))


def cross_attention(h, mem_k, mem_v, w_xq, w_xo):
    Bn, L, D = h.shape
    q = (h @ w_xq).reshape(Bn, L, X_HEADS, X_HD)
    s = jnp.einsum("blhd,bmhd->bhlm", q, mem_k.astype(h.dtype)).astype(jnp.float32) * (X_HD ** -0.5)
    pr = jax.nn.softmax(s, axis=-1).astype(h.dtype)
    o = jnp.einsum("bhlm,bmhd->blhd", pr, mem_v.astype(h.dtype)).reshape(Bn, L, D)
    return o @ w_xo


def moe_ffn(h, w_router, b_router, w1, b1, w2, b2):
    lead = h.shape[:-1]
    D = h.shape[-1]
    t = h.reshape(-1, D)
    T = t.shape[0]
    logits = (t @ w_router + b_router).astype(jnp.float32)
    top_v, top_e = lax.top_k(logits, TOP_K)
    gates = jax.nn.softmax(top_v, axis=-1)
    A = T * TOP_K
    flat_e = top_e.reshape(A).astype(jnp.int32)
    flat_g = gates.reshape(A)
    flat_t = jnp.repeat(jnp.arange(T, dtype=jnp.int32), TOP_K)
    order = jnp.argsort(flat_e)
    se = flat_e[order]
    counts = jnp.bincount(flat_e, length=N_EXPERTS)
    padded = (counts + MOE_BLOCK - 1) // MOE_BLOCK * MOE_BLOCK
    pad_end = jnp.cumsum(padded)
    pad_start = pad_end - padded
    start = jnp.cumsum(counts) - counts
    dest = pad_start[se] + jnp.arange(A, dtype=jnp.int32) - start[se]
    nb = -(-A // MOE_BLOCK) + N_EXPERTS
    R = nb * MOE_BLOCK
    row_tok = jnp.full((R,), T, jnp.int32).at[dest].set(flat_t[order])
    row_gate = jnp.zeros((R,), jnp.float32).at[dest].set(flat_g[order])
    blk_e = jnp.minimum(
        jnp.searchsorted(pad_end, jnp.arange(nb, dtype=jnp.int32) * MOE_BLOCK, side="right"),
        N_EXPERTS - 1).astype(jnp.int32)
    t_pad = jnp.concatenate([t, jnp.zeros((1, D), t.dtype)], axis=0)
    xin = t_pad[row_tok].reshape(nb, MOE_BLOCK, D)

    def expert_block(args):
        xb, e = args
        hh = xb @ w1[e] + b1[e]
        glu, lin = jnp.split(hh, 2, axis=-1)
        glu = jnp.minimum(glu, SWIGLU_LIMIT)
        lin = jnp.clip(lin, -SWIGLU_LIMIT, SWIGLU_LIMIT)
        act = glu * jax.nn.sigmoid(SWIGLU_ALPHA * glu) * (lin + 1.0)
        return act @ w2[e] + b2[e]

    out = lax.map(expert_block, (xin, blk_e)).reshape(R, D)
    y = jax.ops.segment_sum(out * row_gate[:, None].astype(out.dtype), row_tok, num_segments=T + 1)[:T]
    return y.reshape(*lead, D)


def decoder_layer(x, mem_k, mem_v, S0, lb, norm_mix_g, w_in, hgrn_norm_g, gmlp_ln_g, gmlp_ln_b,
                  gmlp_w_s, gmlp_b_s, w_out, norm_x_g, w_xq, w_xo, norm_ffn_g, w_router, b_router,
                  w1, b1, w2, b2):
    y, S_fin, vn = token_mixer(rms_norm(x, norm_mix_g), S0, lb, w_in, w_out, hgrn_norm_g,
                               gmlp_ln_g, gmlp_ln_b, gmlp_w_s, gmlp_b_s)
    x = x + y
    x = x + cross_attention(rms_norm(x, norm_x_g), mem_k, mem_v, w_xq, w_xo)
    x = x + moe_ffn(rms_norm(x, norm_ffn_g), w_router, b_router, w1, b1, w2, b2)
    return x, S_fin, vn


def setup_inputs(seed: int = 0) -> dict:
    key = jax.random.key(seed)
    ks = jax.random.split(key, 32)
    nrm = lambda k, shape, scale: jax.random.normal(k, shape, jnp.float32) * scale
    return {
        "x_prompt": nrm(ks[0], (BATCH, SEQ, D_MODEL), 1.0),
        "x_sample": nrm(ks[1], (DEC_BATCH, DEC_SEQ, D_MODEL), 1.0),
        "mem_prompt": nrm(ks[2], (BATCH, MEM_LEN, D_MODEL), 1.0),
        "state_hgrn": nrm(ks[3], (DEPTH, DEC_BATCH, A_HEADS, A_DK, A_DV), 0.5),
        "cache_mem_k": nrm(ks[4], (DEPTH, DEC_BATCH, MEM_LEN, X_HEADS, X_HD), 1.0),
        "cache_mem_v": nrm(ks[5], (DEPTH, DEC_BATCH, MEM_LEN, X_HEADS, X_HD), 1.0),
        "norm_mix_g": 1.0 + nrm(ks[6], (DEPTH, D_MODEL), 0.02),
        "w_in": nrm(ks[7], (DEPTH, D_MODEL, IN_COLS), D_MODEL ** -0.5),
        "hgrn_lb_logits": nrm(ks[8], (DEPTH + 1, A_WIDTH), 0.1),
        "hgrn_norm_g": 1.0 + nrm(ks[9], (DEPTH, A_WIDTH), 0.02),
        "gmlp_ln_g": 1.0 + nrm(ks[10], (DEPTH, B_GROUPS, B_CH), 0.02),
        "gmlp_ln_b": nrm(ks[11], (DEPTH, B_GROUPS, B_CH), 0.02),
        "gmlp_w_s": nrm(ks[12], (DEPTH, B_GROUPS, GMLP_CHUNK, GMLP_CHUNK), GMLP_CHUNK ** -0.5),
        "gmlp_b_s": 1.0 + nrm(ks[13], (DEPTH, B_GROUPS, GMLP_CHUNK), 0.1),
        "w_out": nrm(ks[14], (DEPTH, MIX_WIDTH, D_MODEL), MIX_WIDTH ** -0.5),
        "norm_x_g": 1.0 + nrm(ks[15], (DEPTH, D_MODEL), 0.02),
        "norm_mem_g": 1.0 + nrm(ks[16], (DEPTH, D_MODEL), 0.02),
        "w_xq": nrm(ks[17], (DEPTH, D_MODEL, D_MODEL), D_MODEL ** -0.5),
        "w_xk": nrm(ks[18], (DEPTH, D_MODEL, D_MODEL), D_MODEL ** -0.5),
        "w_xv": nrm(ks[19], (DEPTH, D_MODEL, D_MODEL), D_MODEL ** -0.5),
        "w_xo": nrm(ks[20], (DEPTH, D_MODEL, D_MODEL), D_MODEL ** -0.5),
        "norm_ffn_g": 1.0 + nrm(ks[21], (DEPTH, D_MODEL), 0.02),
        "w_router": nrm(ks[22], (DEPTH, D_MODEL, N_EXPERTS), D_MODEL ** -0.5),
        "b_router": nrm(ks[23], (DEPTH, N_EXPERTS), 0.01),
        "w1": nrm(ks[24], (DEPTH, N_EXPERTS, D_MODEL, 2 * D_FF), D_MODEL ** -0.5),
        "b1": nrm(ks[25], (DEPTH, N_EXPERTS, 2 * D_FF), 0.01),
        "w2": nrm(ks[26], (DEPTH, N_EXPERTS, D_FF, D_MODEL), D_FF ** -0.5),
        "b2": nrm(ks[27], (DEPTH, N_EXPERTS, D_MODEL), 0.01),
        "norm_final_g": 1.0 + nrm(ks[28], (D_MODEL,), 0.02),
    }


def reference(x_prompt, x_sample, mem_prompt, state_hgrn, cache_mem_k, cache_mem_v,
              norm_mix_g, w_in, hgrn_lb_logits, hgrn_norm_g, gmlp_ln_g, gmlp_ln_b, gmlp_w_s,
              gmlp_b_s, w_out, norm_x_g, norm_mem_g, w_xq, w_xk, w_xv, w_xo, norm_ffn_g,
              w_router, b_router, w1, b1, w2, b2, norm_final_g):
    lb_all = jnp.cumsum(jax.nn.softmax(hgrn_lb_logits.astype(jnp.float32), axis=0), axis=0)
    xp, xs = x_prompt, x_sample
    hgrn_p, mk_p_all, mv_p_all, hgrn_s, gv_s = [], [], [], [], []
    for l in range(DEPTH):
        lw = dict(norm_mix_g=norm_mix_g[l], w_in=w_in[l], hgrn_norm_g=hgrn_norm_g[l],
                  gmlp_ln_g=gmlp_ln_g[l], gmlp_ln_b=gmlp_ln_b[l], gmlp_w_s=gmlp_w_s[l],
                  gmlp_b_s=gmlp_b_s[l], w_out=w_out[l], norm_x_g=norm_x_g[l], w_xq=w_xq[l],
                  w_xo=w_xo[l], norm_ffn_g=norm_ffn_g[l], w_router=w_router[l],
                  b_router=b_router[l], w1=w1[l], b1=b1[l], w2=w2[l], b2=b2[l])
        mk_p, mv_p = memory_kv(mem_prompt, norm_mem_g[l], w_xk[l], w_xv[l])
        S0_p = jnp.zeros((xp.shape[0], A_HEADS, A_DK, A_DV), jnp.float32)
        xp, S_p, _ = decoder_layer(xp, mk_p, mv_p, S0_p, lb_all[l], **lw)
        xs, S_s, vn_s = decoder_layer(xs, cache_mem_k[l], cache_mem_v[l], state_hgrn[l], lb_all[l], **lw)
        hgrn_p.append(S_p)
        mk_p_all.append(mk_p)
        mv_p_all.append(mv_p)
        hgrn_s.append(S_s)
        gv_s.append(vn_s)
    y_prompt = rms_norm(xp, norm_final_g)
    y_sample = rms_norm(xs, norm_final_g)
    return (y_prompt, y_sample, jnp.stack(hgrn_p), jnp.stack(mk_p_all), jnp.stack(mv_p_all),
            jnp.stack(hgrn_s), jnp.stack(gv_s))
```

```python
import functools
import math

import jax
import jax.numpy as jnp
import numpy as np
from jax import lax
from jax.experimental import pallas as pl
from jax.experimental.pallas import tpu as pltpu

F32 = jnp.float32
BF16 = jnp.bfloat16

D_MODEL = 1024
A_HEADS = 4
A_DK = 128
A_WIDTH = 512
B_GROUPS = 4
B_CH = 128
B_WIDTH = 512
IN_COLS = 4 * A_WIDTH + 2 * B_WIDTH
HGRN_CHUNK = 64
GMLP_CHUNK = 128
MEM_LEN = 256
X_HEADS = 4
X_HD = 256
N_EXPERTS = 32
TOP_K = 4
D_FF = 1024
SWIGLU_LIMIT = 7.0
SWIGLU_ALPHA = 1.702
EPS = 1e-6

LANES = 128
VMEM_LIMIT = 56 * 1024 * 1024

NT = (((1,), (1,)), ((), ()))
TN = (((0,), (0,)), ((), ()))


def _dot(a, b):
    return jnp.dot(a, b, preferred_element_type=F32)


def _dot_nt(a, b):
    return lax.dot_general(a, b, NT, preferred_element_type=F32)


def _dot_tn(a, b):
    return lax.dot_general(a, b, TN, preferred_element_type=F32)


def _sigmoid(x):
    return 1.0 / (1.0 + jnp.exp(-x))


def _silu(x):
    return x * _sigmoid(x)


def _gelu(x):
    c = math.sqrt(2.0 / math.pi)
    return x * (0.5 * (1.0 + jnp.tanh(c * (x + 0.044715 * (x * x * x)))))


def _rms(x, g):
    return x * lax.rsqrt(jnp.mean(x * x, axis=-1, keepdims=True) + EPS) * g


def _split_bf16(x):
    hi = x.astype(BF16)
    lo = (x - hi.astype(F32)).astype(BF16)
    return hi, lo


def _params(sem):
    return pltpu.CompilerParams(dimension_semantics=sem, vmem_limit_bytes=VMEM_LIMIT)


def _const_spec(shape):
    nd = len(shape)
    return pl.BlockSpec(shape, lambda *_: (0,) * nd)


def _memkv_kernel(mem_ref, g_ref, wk_ref, wv_ref, k_ref, v_ref, kb_ref, vb_ref):
    mn = _rms(mem_ref[0], g_ref[...]).astype(BF16)
    k = _dot(mn, wk_ref[...])
    v = _dot(mn, wv_ref[...])
    k_ref[0] = k
    v_ref[0] = v
    kb_ref[0] = k.astype(BF16)
    vb_ref[0] = v.astype(BF16)


def _memkv(mem, g, wk_bf, wv_bf):
    nb, m, d = mem.shape
    blk = pl.BlockSpec((1, m, d), lambda b: (b, 0, 0))
    return pl.pallas_call(
        _memkv_kernel,
        grid=(nb,),
        in_specs=[blk, _const_spec((1, d)), _const_spec((d, d)), _const_spec((d, d))],
        out_specs=[blk, blk, blk, blk],
        out_shape=[jax.ShapeDtypeStruct((nb, m, d), F32)] * 2 + [jax.ShapeDtypeStruct((nb, m, d), BF16)] * 2,
        compiler_params=_params(("arbitrary",)),
        name="memkv",
    )(mem, g, wk_bf, wv_bf)


_LEVELS = (32, 16, 8)
_DIAG = 8


def _hgrn_consts():
    c = HGRN_CHUNK
    t = np.arange(c)
    rows = [(t[None, :] <= t[:, None]).astype(np.float32)]
    masks = []
    for h in _LEVELS:
        m = np.zeros((c, c), np.float32)
        for r in range(c):
            ref = (r // (2 * h)) * 2 * h + h - 1
            if r % (2 * h) >= h:
                m[r, ref + 1:r + 1] = 1.0
            else:
                m[r, r + 1:ref + 1] = 1.0
        rows.append(m)
        tt, ss = t[:, None], t[None, :]
        masks.append(((tt // (2 * h) == ss // (2 * h)) & (tt % (2 * h) >= h) & (ss % (2 * h) < h)).astype(np.float32))
    rows.append((t[None, :] > t[:, None]).astype(np.float32))
    return np.concatenate(rows, 0), np.stack(masks)


def _lower_bound(lbl):
    mx = jnp.max(lbl, axis=0, keepdims=True)
    e = jnp.exp(lbl - mx)
    return e[0:1] / jnp.sum(e, axis=0, keepdims=True)


def _hgrn_gates(p, lb):
    q = _silu(p[:, 0:A_WIDTH])
    fg = lb + (1.0 - lb) * _sigmoid(p[:, A_WIDTH:2 * A_WIDTH])
    return q, 1.0 - fg, jnp.log(fg), p[:, 2 * A_WIDTH:3 * A_WIDTH]


def _head_norm_gate(o, hng, g):
    outs = []
    for hd in range(A_HEADS):
        sl = slice(hd * A_DK, (hd + 1) * A_DK)
        oh = o[:, sl]
        outs.append(oh * lax.rsqrt(jnp.mean(oh * oh, axis=-1, keepdims=True) + EPS) * hng[:, sl])
    return jnp.concatenate(outs, axis=-1) * _silu(g)


def _gmlp_norm(v, lng, lnb):
    vf = _gelu(v)
    outs = []
    for gi in range(B_GROUPS):
        sl = slice(gi * B_CH, (gi + 1) * B_CH)
        vg = vf[:, sl]
        mu = jnp.mean(vg, axis=-1, keepdims=True)
        dv = vg - mu
        var = jnp.mean(dv * dv, axis=-1, keepdims=True)
        outs.append(dv * lax.rsqrt(var + EPS) * lng[:, sl] + lnb[:, sl])
    return jnp.concatenate(outs, axis=-1)


def _mixer_kernel(x_ref, s0_ref, gmix_ref, win_ref, lbl_ref, mst_ref, msk_ref, hng_ref, lng_ref, lnb_ref,
                  ws_ref, bsb_ref, wout_ref, x1_ref, sfin_ref, vn_ref,
                  st_ref, q_s, k_s, v_s, lf_s, oa_s, *, tl, nvalid, per_chunk, gchunk):
    j = pl.program_id(1)
    c = HGRN_CHUNK

    if not per_chunk:
        @pl.when(j == 0)
        def _():
            for hd in range(A_HEADS):
                st_ref[hd] = s0_ref[0, hd].T

    x = x_ref[0]
    h = _rms(x, gmix_ref[...]).astype(BF16)
    p = _dot(h, win_ref[...])
    lb = _lower_bound(lbl_ref[...])
    q, k, lf, v = _hgrn_gates(p, lb)
    if nvalid < c:
        row = lax.broadcasted_iota(jnp.int32, (tl, 1), 0)
        lf = jnp.where(row % c < nvalid, lf, 0.0)
    q_s[...] = q
    k_s[...] = k
    lf_s[...] = lf
    v_s[...] = v

    sub = lax.broadcasted_iota(jnp.int32, (c // _DIAG, _DIAG, 1), 1)

    def chunk(ci, carry):
        r0 = pl.multiple_of(ci * c, c)
        lfc = lf_s[pl.ds(r0, c), :]
        hi, lo = _split_bf16(lfc)
        mst = mst_ref[...]
        bexp = _dot(mst, hi) + _dot(mst, lo)
        e = jnp.exp(bexp)
        qc = q_s[pl.ds(r0, c), :]
        kc = k_s[pl.ds(r0, c), :]
        vc = v_s[pl.ds(r0, c), :]
        for hd in range(A_HEADS):
            sl = slice(hd * A_DK, (hd + 1) * A_DK)
            qh, kh, vh, eh = qc[:, sl], kc[:, sl], vc[:, sl], e[:, sl]
            vb = vh.astype(BF16)
            st = s0_ref[ci, hd].T if per_chunk else st_ref[hd]
            sc = jnp.zeros((c, c), F32)
            for li in range(len(_LEVELS)):
                el = eh[(li + 1) * c:(li + 2) * c]
                sc = sc + _dot_nt((qh * el).astype(BF16), (kh * el).astype(BF16)) * msk_ref[li]
            o = _dot(sc.astype(BF16), vb) + _dot_nt((qh * eh[0:c]).astype(BF16), st.astype(BF16))
            nblk = c // _DIAG
            b3 = bexp[0:c, sl].reshape(nblk, _DIAG, A_DK)
            q3 = qh.reshape(nblk, _DIAG, A_DK)
            k3 = kh.reshape(nblk, _DIAG, A_DK)
            v3 = vh.reshape(nblk, _DIAG, A_DK)
            od = jnp.zeros((nblk, _DIAG, A_DK), F32)
            for s in range(_DIAG):
                dec = jnp.exp(jnp.minimum(b3 - b3[:, s:s + 1, :], 0.0))
                w = jnp.sum(q3 * dec * k3[:, s:s + 1, :], axis=-1, keepdims=True)
                w = jnp.where(sub >= s, w, 0.0)
                od = od + w * v3[:, s:s + 1, :]
            oa_s[pl.ds(r0, c), sl] = o + od.reshape(c, A_DK)
            kd = (kh * eh[4 * c:5 * c]).astype(BF16)
            st_new = st * eh[c - 1:c] + _dot_tn(vb, kd)
            if per_chunk:
                sfin_ref[ci, hd] = st_new.T
            else:
                st_ref[hd] = st_new
        return carry

    lax.fori_loop(0, tl // c, chunk, 0)

    o_a = _head_norm_gate(oa_s[...], hng_ref[...], p[:, 3 * A_WIDTH:4 * A_WIDTH])
    ug = _gelu(p[:, 4 * A_WIDTH:4 * A_WIDTH + B_WIDTH])
    vn = _gmlp_norm(p[:, 4 * A_WIDTH + B_WIDTH:], lng_ref[...], lnb_ref[...])
    vn_ref[0] = vn
    sg = []
    for gi in range(B_GROUPS):
        sl = slice(gi * B_CH, (gi + 1) * B_CH)
        wsg = ws_ref[gi][0:gchunk, 0:gchunk]
        bsg = bsb_ref[gi][0:gchunk, :]
        rows = []
        for cc in range(tl // gchunk):
            vcc = vn[cc * gchunk:(cc + 1) * gchunk, sl].astype(BF16)
            rows.append(_dot(wsg, vcc) + bsg)
        sg.append(jnp.concatenate(rows, axis=0))
    o_b = ug * jnp.concatenate(sg, axis=-1)
    y = _dot(o_a.astype(BF16), wout_ref[0:A_WIDTH, :]) + _dot(o_b.astype(BF16), wout_ref[A_WIDTH:, :])
    x1_ref[0] = x + y

    if not per_chunk:
        @pl.when(j == pl.num_programs(1) - 1)
        def _():
            for hd in range(A_HEADS):
                sfin_ref[0, hd] = st_ref[hd].T


def _mixer_weights(norm_mix_g, w_in, lb_logits, hgrn_norm_g, ln_g, ln_b, w_s, b_s, w_out):
    mst, msk = _hgrn_consts()
    ws_bf = jnp.tril(w_s).astype(BF16)
    bsb = jnp.broadcast_to(b_s[:, :, None], (B_GROUPS, GMLP_CHUNK, B_CH))
    return (norm_mix_g.reshape(1, D_MODEL), w_in.astype(BF16), lb_logits, jnp.asarray(mst, BF16), jnp.asarray(msk, F32),
            hgrn_norm_g.reshape(1, A_WIDTH), ln_g.reshape(1, B_WIDTH), ln_b.reshape(1, B_WIDTH), ws_bf, bsb,
            w_out.astype(BF16))


def _mixer(x, s0, weights, *, tl, nvalid, per_chunk):
    nb, seq, d = x.shape
    c = HGRN_CHUNK
    spb = tl // c if per_chunk else 1
    gchunk = c if per_chunk else GMLP_CHUNK
    blk = pl.BlockSpec((1, tl, d), lambda b, j: (b, j, 0))
    sblk = pl.BlockSpec((spb, A_HEADS, A_DK, A_DK), lambda b, j: (b, 0, 0, 0))
    act = pltpu.VMEM((tl, A_WIDTH), F32)
    return pl.pallas_call(
        functools.partial(_mixer_kernel, tl=tl, nvalid=nvalid, per_chunk=per_chunk, gchunk=gchunk),
        grid=(nb, seq // tl),
        in_specs=[blk, sblk] + [_const_spec(w.shape) for w in weights],
        out_specs=[blk, sblk, pl.BlockSpec((1, tl, B_WIDTH), lambda b, j: (b, j, 0))],
        out_shape=[jax.ShapeDtypeStruct((nb, seq, d), F32),
                   jax.ShapeDtypeStruct(s0.shape, F32),
                   jax.ShapeDtypeStruct((nb, seq, B_WIDTH), F32)],
        scratch_shapes=[pltpu.VMEM((A_HEADS, A_DK, A_DK), F32), act, act, act, act, act],
        compiler_params=_params(("arbitrary", "arbitrary")),
        name="mixer",
    )(x, s0, *weights)


def _attn_kernel(x_ref, k_ref, v_ref, gx_ref, wq_ref, wo_ref, gf_ref, wr_ref, br_ref, x2_ref, hn_ref, lg_ref):
    x = x_ref[0]
    h = _rms(x, gx_ref[...]).astype(BF16)
    q = _dot(h, wq_ref[...])
    kb = k_ref[0].astype(BF16)
    vb = v_ref[0].astype(BF16)
    outs = []
    for hd in range(X_HEADS):
        sl = slice(hd * X_HD, (hd + 1) * X_HD)
        s = _dot_nt(q[:, sl].astype(BF16), kb[:, sl]) * (X_HD ** -0.5)
        pexp = jnp.exp(s - jnp.max(s, axis=-1, keepdims=True))
        pr = pexp / jnp.sum(pexp, axis=-1, keepdims=True)
        outs.append(_dot(pr.astype(BF16), vb[:, sl]))
    o = jnp.concatenate(outs, axis=-1).astype(BF16)
    x2 = x + _dot(o, wo_ref[...])
    x2_ref[0] = x2
    hn = _rms(x2, gf_ref[...]).astype(BF16)
    hn_ref[0] = hn
    lg_ref[0] = _dot(hn, wr_ref[...]) + br_ref[...]


def _attn_weights(norm_x_g, w_xq, w_xo, norm_ffn_g, w_router, b_router):
    wr = jnp.zeros((D_MODEL, LANES), BF16).at[:, :N_EXPERTS].set(w_router.astype(BF16))
    br = jnp.full((1, LANES), -1e30, F32).at[0, :N_EXPERTS].set(b_router)
    return (norm_x_g.reshape(1, D_MODEL), w_xq.astype(BF16), w_xo.astype(BF16), norm_ffn_g.reshape(1, D_MODEL), wr, br)


def _attn(x, mem_k, mem_v, weights, *, tl):
    nb, seq, d = x.shape
    blk = pl.BlockSpec((1, tl, d), lambda b, j: (b, j, 0))
    kv = pl.BlockSpec((1, MEM_LEN, d), lambda b, j: (b, 0, 0))
    return pl.pallas_call(
        _attn_kernel,
        grid=(nb, seq // tl),
        in_specs=[blk, kv, kv] + [_const_spec(w.shape) for w in weights],
        out_specs=[blk, blk, pl.BlockSpec((1, tl, LANES), lambda b, j: (b, j, 0))],
        out_shape=[jax.ShapeDtypeStruct((nb, seq, d), F32), jax.ShapeDtypeStruct((nb, seq, d), BF16),
                   jax.ShapeDtypeStruct((nb, seq, LANES), F32)],
        compiler_params=_params(("arbitrary", "arbitrary")),
        name="attn",
    )(x, mem_k, mem_v, *weights)


def _route_kernel(lg_ref, gm_ref):
    work = lg_ref[...]
    lane = lax.broadcasted_iota(jnp.int32, work.shape, 1).astype(F32)
    vals, hots = [], []
    for _ in range(TOP_K):
        m = jnp.max(work, axis=-1, keepdims=True)
        idx = jnp.min(jnp.where(work == m, lane, float(LANES)), axis=-1, keepdims=True)
        hot = lane == idx
        vals.append(m)
        hots.append(hot)
        work = jnp.where(hot, -jnp.inf, work)
    es = [jnp.exp(v - vals[0]) for v in vals]
    tot = es[0] + es[1] + es[2] + es[3]
    gm = jnp.zeros(work.shape, F32)
    for hot, e in zip(hots, es):
        gm = jnp.where(hot, e / tot, gm)
    gm_ref[...] = gm


def _route(logits, tr=512):
    t = logits.shape[0]
    blk = pl.BlockSpec((tr, LANES), lambda i: (i, 0))
    return pl.pallas_call(
        _route_kernel, grid=(t // tr,), in_specs=[blk], out_specs=blk,
        out_shape=jax.ShapeDtypeStruct((t, LANES), F32),
        compiler_params=_params(("arbitrary",)), name="route",
    )(logits)


MOE_TB = 1536
MOE_CAP = 256


def _moe_kernel(xn_ref, gm_ref, w1_ref, b1_ref, w2_ref, b2_ref, y_ref, g_s, c_s, gt_s, ct_s):
    e = pl.program_id(1)
    tb, cap = MOE_TB, MOE_CAP

    @pl.when(e == 0)
    def _():
        gm = gm_ref[...]
        hot = (gm > 0.0).astype(BF16)
        r = lax.broadcasted_iota(jnp.int32, (tb, tb), 0)
        cidx = lax.broadcasted_iota(jnp.int32, (tb, tb), 1)
        low = (cidx < r).astype(BF16)
        g_s[...] = gm
        c_s[...] = _dot(low, hot)
        gt = gm.T
        gt_s[...] = gt
        up = (r < cidx).astype(BF16)
        ct_s[...] = _dot((gt > 0.0).astype(BF16), up)
        y_ref[...] = jnp.zeros_like(y_ref)

    lane = lax.broadcasted_iota(jnp.int32, (tb, LANES), 1)
    pick = lane == e
    g_col = jnp.sum(jnp.where(pick, g_s[...], 0.0), axis=-1, keepdims=True)
    c_col = jnp.sum(jnp.where(pick, c_s[...], 0.0), axis=-1, keepdims=True)
    g_row = gt_s[pl.ds(e, 1), :]
    c_row = ct_s[pl.ds(e, 1), :]
    n = jnp.sum((g_row > 0.0).astype(F32)).astype(jnp.int32)
    sub_c = lax.broadcasted_iota(jnp.int32, (cap, 1), 0).astype(F32)
    lane_c = lax.broadcasted_iota(jnp.int32, (1, cap), 1).astype(F32)

    def one_pass(s, carry):
        base = (s * cap).astype(F32)
        hit = (g_row > 0.0) & (c_row - base == sub_c)
        p = jnp.where(hit, 1.0, 0.0).astype(BF16)
        g_rows = jnp.sum(jnp.where(hit, g_row, 0.0), axis=-1, keepdims=True)
        xin = _dot(p, xn_ref[...]).astype(BF16)
        hh = _dot(xin, w1_ref[0]) + b1_ref[0]
        glu = jnp.minimum(hh[:, :D_FF], SWIGLU_LIMIT)
        lin = jnp.clip(hh[:, D_FF:], -SWIGLU_LIMIT, SWIGLU_LIMIT)
        act = glu * _sigmoid(SWIGLU_ALPHA * glu) * (lin + 1.0)
        out = (_dot(act.astype(BF16), w2_ref[0]) + b2_ref[0]) * g_rows
        hit_t = (g_col > 0.0) & (c_col - base == lane_c)
        pt = jnp.where(hit_t, 1.0, 0.0).astype(BF16)
        y_ref[...] += _dot(pt, out.astype(BF16))
        return carry

    lax.fori_loop(0, (n + cap - 1) // cap, one_pass, 0)


def _moe(xn, gm, w1_bf, b1, w2_bf, b2):
    t, d = xn.shape
    tb = MOE_TB
    return pl.pallas_call(
        _moe_kernel,
        grid=(t // tb, N_EXPERTS),
        in_specs=[pl.BlockSpec((tb, d), lambda i, e: (i, 0)),
                  pl.BlockSpec((tb, LANES), lambda i, e: (i, 0)),
                  pl.BlockSpec((1, d, 2 * D_FF), lambda i, e: (e, 0, 0)),
                  pl.BlockSpec((1, 1, 2 * D_FF), lambda i, e: (e, 0, 0)),
                  pl.BlockSpec((1, D_FF, d), lambda i, e: (e, 0, 0)),
                  pl.BlockSpec((1, 1, d), lambda i, e: (e, 0, 0))],
        out_specs=pl.BlockSpec((tb, d), lambda i, e: (i, 0)),
        out_shape=jax.ShapeDtypeStruct((t, d), F32),
        scratch_shapes=[pltpu.VMEM((tb, LANES), F32), pltpu.VMEM((tb, LANES), F32),
                        pltpu.VMEM((LANES, tb), F32), pltpu.VMEM((LANES, tb), F32)],
        compiler_params=_params(("arbitrary", "arbitrary")),
        name="moe",
    )(xn, gm, w1_bf, b1, w2_bf, b2)


def _final_kernel(x_ref, y_ref, g_ref, o_ref):
    o_ref[...] = _rms(x_ref[...] + y_ref[...], g_ref[...])


def _final(x2, y_all, row0, g, tf=512):
    t, d = x2.shape
    off = row0 // tf
    return pl.pallas_call(
        _final_kernel, grid=(t // tf,),
        in_specs=[pl.BlockSpec((tf, d), lambda i: (i, 0)), pl.BlockSpec((tf, d), lambda i: (i + off, 0)),
                  _const_spec((1, d))],
        out_specs=pl.BlockSpec((tf, d), lambda i: (i, 0)),
        out_shape=jax.ShapeDtypeStruct((t, d), F32),
        compiler_params=_params(("arbitrary",)), name="final",
    )(x2, y_all, g)


SAMPLE_PAD = HGRN_CHUNK
STEP_ROWS = 256


def kernel(x_prompt, x_sample, mem_prompt, state_hgrn, cache_mem_k, cache_mem_v, norm_mix_g, w_in, hgrn_lb_logits,
           hgrn_norm_g, gmlp_ln_g, gmlp_ln_b, gmlp_w_s, gmlp_b_s, w_out, norm_x_g, norm_mem_g, w_xq, w_xk, w_xv, w_xo,
           norm_ffn_g, w_router, b_router, w1, b1, w2, b2, norm_final_g):
    assert w_in.shape[0] == 1, "one layer"
    nbp, seq, d = x_prompt.shape
    nbs, dseq, _ = x_sample.shape
    mw = _mixer_weights(norm_mix_g[0], w_in[0], hgrn_lb_logits, hgrn_norm_g[0], gmlp_ln_g[0], gmlp_ln_b[0],
                        gmlp_w_s[0], gmlp_b_s[0], w_out[0])
    aw = _attn_weights(norm_x_g[0], w_xq[0], w_xo[0], norm_ffn_g[0], w_router[0], b_router[0])

    mk, mv, mkb, mvb = _memkv(mem_prompt, norm_mem_g[0].reshape(1, d), w_xk[0].astype(BF16), w_xv[0].astype(BF16))
    s0p = jnp.zeros((nbp, A_HEADS, A_DK, A_DK), F32)
    x1p, s_p, _ = _mixer(x_prompt, s0p, mw, tl=STEP_ROWS, nvalid=HGRN_CHUNK, per_chunk=False)
    x2p, hnp, lgp = _attn(x1p, mkb, mvb, aw, tl=STEP_ROWS)

    spb = STEP_ROWS // SAMPLE_PAD
    xs = jnp.pad(x_sample, ((0, 0), (0, SAMPLE_PAD - dseq), (0, 0))).reshape(nbs // spb, STEP_ROWS, d)
    x1s, s_s, vns = _mixer(xs, state_hgrn[0], mw, tl=STEP_ROWS, nvalid=dseq, per_chunk=True)
    x2s, hns, lgs = _attn(x1s.reshape(nbs, SAMPLE_PAD, d), cache_mem_k[0].reshape(nbs, MEM_LEN, d),
                          cache_mem_v[0].reshape(nbs, MEM_LEN, d), aw, tl=SAMPLE_PAD)

    def real(a):
        return a.reshape(nbs, SAMPLE_PAD, a.shape[-1])[:, :dseq].reshape(nbs * dseq, a.shape[-1])

    tp = nbp * seq

    xn = jnp.concatenate([hnp.reshape(tp, d), real(hns)], axis=0)
    gm = _route(jnp.concatenate([lgp.reshape(tp, LANES), real(lgs)], axis=0))
    y_all = _moe(xn, gm, w1[0].astype(BF16), b1[0].reshape(N_EXPERTS, 1, 2 * D_FF), w2[0].astype(BF16),
                 b2[0].reshape(N_EXPERTS, 1, d))
    gfin = norm_final_g.reshape(1, d)
    y_prompt = _final(x2p.reshape(tp, d), y_all, 0, gfin).reshape(nbp, seq, d)
    y_sample = _final(real(x2s), y_all, tp, gfin).reshape(nbs, dseq, d)

    vn_s = real(vns).reshape(1, nbs, dseq, B_GROUPS, B_CH)
    return (y_prompt, y_sample, s_p[None], mk.reshape(1, nbp, MEM_LEN, X_HEADS, X_HD),
            mv.reshape(1, nbp, MEM_LEN, X_HEADS, X_HD), s_s[None], vn_s)
```

```python
import functools
import math

import jax
import jax.numpy as jnp
import numpy as np
from jax import lax
from jax.experimental import pallas as pl
from jax.experimental.pallas import tpu as pltpu

F32 = jnp.float32
BF16 = jnp.bfloat16

D_MODEL = 1024
A_HEADS = 4
A_DK = 128
A_WIDTH = 512
B_GROUPS = 4
B_CH = 128
B_WIDTH = 512
IN_COLS = 4 * A_WIDTH + 2 * B_WIDTH
HGRN_CHUNK = 64
GMLP_CHUNK = 128
MEM_LEN = 256
X_HEADS = 4
X_HD = 256
N_EXPERTS = 32
TOP_K = 4
D_FF = 1024
SWIGLU_LIMIT = 7.0
SWIGLU_ALPHA = 1.702
EPS = 1e-6

LANES = 128
VMEM_LIMIT = 56 * 1024 * 1024

NT = (((1,), (1,)), ((), ()))
TN = (((0,), (0,)), ((), ()))


def _dot(a, b):
    return jnp.dot(a, b, preferred_element_type=F32)


def _dot_nt(a, b):
    return lax.dot_general(a, b, NT, preferred_element_type=F32)


def _dot_tn(a, b):
    return lax.dot_general(a, b, TN, preferred_element_type=F32)


def _sigmoid(x):
    return 1.0 / (1.0 + jnp.exp(-x))


def _silu(x):
    return x * _sigmoid(x)


def _gelu(x):
    c = math.sqrt(2.0 / math.pi)
    return x * (0.5 * (1.0 + jnp.tanh(c * (x + 0.044715 * (x * x * x)))))


def _rms(x, g):
    return x * lax.rsqrt(jnp.mean(x * x, axis=-1, keepdims=True) + EPS) * g


def _split_bf16(x):
    hi = x.astype(BF16)
    lo = (x - hi.astype(F32)).astype(BF16)
    return hi, lo


def _params(sem):
    return pltpu.CompilerParams(dimension_semantics=sem, vmem_limit_bytes=VMEM_LIMIT)


def _const_spec(shape):
    nd = len(shape)
    return pl.BlockSpec(shape, lambda *_: (0,) * nd)


def _memkv_kernel(mem_ref, g_ref, wk_ref, wv_ref, k_ref, v_ref, kb_ref, vb_ref):
    mn = _rms(mem_ref[0], g_ref[...]).astype(BF16)
    k = _dot(mn, wk_ref[...])
    v = _dot(mn, wv_ref[...])
    k_ref[0] = k
    v_ref[0] = v
    kb_ref[0] = k.astype(BF16)
    vb_ref[0] = v.astype(BF16)


def _memkv(mem, g, wk_bf, wv_bf):
    nb, m, d = mem.shape
    blk = pl.BlockSpec((1, m, d), lambda b: (b, 0, 0))
    return pl.pallas_call(
        _memkv_kernel,
        grid=(nb,),
        in_specs=[blk, _const_spec((1, d)), _const_spec((d, d)), _const_spec((d, d))],
        out_specs=[blk, blk, blk, blk],
        out_shape=[jax.ShapeDtypeStruct((nb, m, d), F32)] * 2 + [jax.ShapeDtypeStruct((nb, m, d), BF16)] * 2,
        compiler_params=_params(("arbitrary",)),
        name="memkv",
    )(mem, g, wk_bf, wv_bf)


_LEVELS = (32, 16, 8)
_DIAG = 8


def _hgrn_consts():
    c = HGRN_CHUNK
    t = np.arange(c)
    rows = [(t[None, :] <= t[:, None]).astype(np.float32)]
    masks = []
    for h in _LEVELS:
        m = np.zeros((c, c), np.float32)
        for r in range(c):
            ref = (r // (2 * h)) * 2 * h + h - 1
            if r % (2 * h) >= h:
                m[r, ref + 1:r + 1] = 1.0
            else:
                m[r, r + 1:ref + 1] = 1.0
        rows.append(m)
        tt, ss = t[:, None], t[None, :]
        masks.append(((tt // (2 * h) == ss // (2 * h)) & (tt % (2 * h) >= h) & (ss % (2 * h) < h)).astype(np.float32))
    rows.append((t[None, :] > t[:, None]).astype(np.float32))
    return np.concatenate(rows, 0), np.stack(masks)


def _lower_bound(lbl):
    mx = jnp.max(lbl, axis=0, keepdims=True)
    e = jnp.exp(lbl - mx)
    return e[0:1] / jnp.sum(e, axis=0, keepdims=True)


def _hgrn_gates(p, lb):
    q = _silu(p[:, 0:A_WIDTH])
    fg = lb + (1.0 - lb) * _sigmoid(p[:, A_WIDTH:2 * A_WIDTH])
    return q, 1.0 - fg, jnp.log(fg), p[:, 2 * A_WIDTH:3 * A_WIDTH]


def _head_norm_gate(o, hng, g):
    outs = []
    for hd in range(A_HEADS):
        sl = slice(hd * A_DK, (hd + 1) * A_DK)
        oh = o[:, sl]
        outs.append(oh * lax.rsqrt(jnp.mean(oh * oh, axis=-1, keepdims=True) + EPS) * hng[:, sl])
    return jnp.concatenate(outs, axis=-1) * _silu(g)


def _gmlp_norm(v, lng, lnb):
    vf = _gelu(v)
    outs = []
    for gi in range(B_GROUPS):
        sl = slice(gi * B_CH, (gi + 1) * B_CH)
        vg = vf[:, sl]
        mu = jnp.mean(vg, axis=-1, keepdims=True)
        dv = vg - mu
        var = jnp.mean(dv * dv, axis=-1, keepdims=True)
        outs.append(dv * lax.rsqrt(var + EPS) * lng[:, sl] + lnb[:, sl])
    return jnp.concatenate(outs, axis=-1)


def _mixer_kernel(x_ref, s0_ref, gmix_ref, win_ref, lbl_ref, mst_ref, msk_ref, hng_ref, lng_ref, lnb_ref,
                  ws_ref, bsb_ref, wout_ref, x1_ref, sfin_ref, vn_ref,
                  st_ref, q_s, k_s, v_s, lf_s, oa_s, *, tl, nvalid, per_chunk, gchunk):
    j = pl.program_id(1)
    c = HGRN_CHUNK

    if not per_chunk:
        @pl.when(j == 0)
        def _():
            for hd in range(A_HEADS):
                st_ref[hd] = s0_ref[0, hd].T

    x = x_ref[0]
    h = _rms(x, gmix_ref[...]).astype(BF16)
    p = _dot(h, win_ref[...])
    lb = _lower_bound(lbl_ref[...])
    q, k, lf, v = _hgrn_gates(p, lb)
    if nvalid < c:
        row = lax.broadcasted_iota(jnp.int32, (tl, 1), 0)
        lf = jnp.where(row % c < nvalid, lf, 0.0)
    q_s[...] = q
    k_s[...] = k
    lf_s[...] = lf
    v_s[...] = v

    sub = lax.broadcasted_iota(jnp.int32, (c // _DIAG, _DIAG, 1), 1)

    def chunk(ci, carry):
        r0 = pl.multiple_of(ci * c, c)
        lfc = lf_s[pl.ds(r0, c), :]
        hi, lo = _split_bf16(lfc)
        mst = mst_ref[...]
        bexp = _dot(mst, hi) + _dot(mst, lo)
        e = jnp.exp(bexp)
        qc = q_s[pl.ds(r0, c), :]
        kc = k_s[pl.ds(r0, c), :]
        vc = v_s[pl.ds(r0, c), :]
        for hd in range(A_HEADS):
            sl = slice(hd * A_DK, (hd + 1) * A_DK)
            qh, kh, vh, eh = qc[:, sl], kc[:, sl], vc[:, sl], e[:, sl]
            vb = vh.astype(BF16)
            st = s0_ref[ci, hd].T if per_chunk else st_ref[hd]
            sc = jnp.zeros((c, c), F32)
            for li in range(len(_LEVELS)):
                el = eh[(li + 1) * c:(li + 2) * c]
                sc = sc + _dot_nt((qh * el).astype(BF16), (kh * el).astype(BF16)) * msk_ref[li]
            o = _dot(sc.astype(BF16), vb) + _dot_nt((qh * eh[0:c]).astype(BF16), st.astype(BF16))
            nblk = c // _DIAG
            b3 = bexp[0:c, sl].reshape(nblk, _DIAG, A_DK)
            q3 = qh.reshape(nblk, _DIAG, A_DK)
            k3 = kh.reshape(nblk, _DIAG, A_DK)
            v3 = vh.reshape(nblk, _DIAG, A_DK)
            od = jnp.zeros((nblk, _DIAG, A_DK), F32)
            for s in range(_DIAG):
                dec = jnp.exp(jnp.minimum(b3 - b3[:, s:s + 1, :], 0.0))
                w = jnp.sum(q3 * dec * k3[:, s:s + 1, :], axis=-1, keepdims=True)
                w = jnp.where(sub >= s, w, 0.0)
                od = od + w * v3[:, s:s + 1, :]
            oa_s[pl.ds(r0, c), sl] = o + od.reshape(c, A_DK)
            kd = (kh * eh[4 * c:5 * c]).astype(BF16)
            st_new = st * eh[c - 1:c] + _dot_tn(vb, kd)
            if per_chunk:
                sfin_ref[ci, hd] = st_new.T
            else:
                st_ref[hd] = st_new
        return carry

    lax.fori_loop(0, tl // c, chunk, 0)

    o_a = _head_norm_gate(oa_s[...], hng_ref[...], p[:, 3 * A_WIDTH:4 * A_WIDTH])
    ug = _gelu(p[:, 4 * A_WIDTH:4 * A_WIDTH + B_WIDTH])
    vn = _gmlp_norm(p[:, 4 * A_WIDTH + B_WIDTH:], lng_ref[...], lnb_ref[...])
    vn_ref[0] = vn
    sg = []
    for gi in range(B_GROUPS):
        sl = slice(gi * B_CH, (gi + 1) * B_CH)
        wsg = ws_ref[gi][0:gchunk, 0:gchunk]
        bsg = bsb_ref[gi][0:gchunk, :]
        rows = []
        for cc in range(tl // gchunk):
            vcc = vn[cc * gchunk:(cc + 1) * gchunk, sl].astype(BF16)
            rows.append(_dot(wsg, vcc) + bsg)
        sg.append(jnp.concatenate(rows, axis=0))
    o_b = ug * jnp.concatenate(sg, axis=-1)
    y = _dot(o_a.astype(BF16), wout_ref[0:A_WIDTH, :]) + _dot(o_b.astype(BF16), wout_ref[A_WIDTH:, :])
    x1_ref[0] = x + y

    if not per_chunk:
        @pl.when(j == pl.num_programs(1) - 1)
        def _():
            for hd in range(A_HEADS):
                sfin_ref[0, hd] = st_ref[hd].T


def _mixer_weights(norm_mix_g, w_in, lb_logits, hgrn_norm_g, ln_g, ln_b, w_s, b_s, w_out):
    mst, msk = _hgrn_consts()
    ws_bf = jnp.tril(w_s).astype(BF16)
    bsb = jnp.broadcast_to(b_s[:, :, None], (B_GROUPS, GMLP_CHUNK, B_CH))
    return (norm_mix_g.reshape(1, D_MODEL), w_in.astype(BF16), lb_logits, jnp.asarray(mst, BF16), jnp.asarray(msk, F32),
            hgrn_norm_g.reshape(1, A_WIDTH), ln_g.reshape(1, B_WIDTH), ln_b.reshape(1, B_WIDTH), ws_bf, bsb,
            w_out.astype(BF16))


def _mixer(x, s0, weights, *, tl, nvalid, per_chunk):
    nb, seq, d = x.shape
    c = HGRN_CHUNK
    spb = tl // c if per_chunk else 1
    gchunk = c if per_chunk else GMLP_CHUNK
    blk = pl.BlockSpec((1, tl, d), lambda b, j: (b, j, 0))
    sblk = pl.BlockSpec((spb, A_HEADS, A_DK, A_DK), lambda b, j: (b, 0, 0, 0))
    act = pltpu.VMEM((tl, A_WIDTH), F32)
    return pl.pallas_call(
        functools.partial(_mixer_kernel, tl=tl, nvalid=nvalid, per_chunk=per_chunk, gchunk=gchunk),
        grid=(nb, seq // tl),
        in_specs=[blk, sblk] + [_const_spec(w.shape) for w in weights],
        out_specs=[blk, sblk, pl.BlockSpec((1, tl, B_WIDTH), lambda b, j: (b, j, 0))],
        out_shape=[jax.ShapeDtypeStruct((nb, seq, d), F32),
                   jax.ShapeDtypeStruct(s0.shape, F32),
                   jax.ShapeDtypeStruct((nb, seq, B_WIDTH), F32)],
        scratch_shapes=[pltpu.VMEM((A_HEADS, A_DK, A_DK), F32), act, act, act, act, act],
        compiler_params=_params(("arbitrary", "arbitrary")),
        name="mixer",
    )(x, s0, *weights)


def _kv_head(ref, g, hd, by_head):
    if by_head:
        return ref[0, g, :, hd, :].astype(BF16)
    return ref[g, :, hd * X_HD:(hd + 1) * X_HD].astype(BF16)


def _attn_kernel(x_ref, k_ref, v_ref, gx_ref, wq_ref, wo_ref, gf_ref, wr_ref, br_ref, x2_ref, hn_ref, lg_ref,
                 *, nseq, by_head):
    x = x_ref[0]
    rows = x.shape[0] // nseq
    h = _rms(x, gx_ref[...]).astype(BF16)
    q = _dot(h, wq_ref[...])
    per_seq = []
    for g in range(nseq):
        qg = q[g * rows:(g + 1) * rows]
        outs = []
        for hd in range(X_HEADS):
            sl = slice(hd * X_HD, (hd + 1) * X_HD)
            s = _dot_nt(qg[:, sl].astype(BF16), _kv_head(k_ref, g, hd, by_head)) * (X_HD ** -0.5)
            pexp = jnp.exp(s - jnp.max(s, axis=-1, keepdims=True))
            pr = pexp / jnp.sum(pexp, axis=-1, keepdims=True)
            outs.append(_dot(pr.astype(BF16), _kv_head(v_ref, g, hd, by_head)))
        per_seq.append(jnp.concatenate(outs, axis=-1))
    o = jnp.concatenate(per_seq, axis=0).astype(BF16)
    x2 = x + _dot(o, wo_ref[...])
    x2_ref[0] = x2
    hn = _rms(x2, gf_ref[...]).astype(BF16)
    hn_ref[0] = hn
    lg_ref[0] = _dot(hn, wr_ref[...]) + br_ref[...]


def _attn_weights(norm_x_g, w_xq, w_xo, norm_ffn_g, w_router, b_router):
    wr = jnp.zeros((D_MODEL, LANES), BF16).at[:, :N_EXPERTS].set(w_router.astype(BF16))
    br = jnp.full((1, LANES), -1e30, F32).at[0, :N_EXPERTS].set(b_router)
    return (norm_x_g.reshape(1, D_MODEL), w_xq.astype(BF16), w_xo.astype(BF16), norm_ffn_g.reshape(1, D_MODEL), wr, br)


def _attn(x, mem_k, mem_v, weights, *, tl, nseq):
    nb, seq, d = x.shape
    by_head = mem_k.ndim == 5
    blk = pl.BlockSpec((1, tl, d), lambda b, j: (b, j, 0))
    if by_head:
        kv = pl.BlockSpec((1, nseq, MEM_LEN, X_HEADS, X_HD), lambda b, j: (0, b, 0, 0, 0))
    else:
        kv = pl.BlockSpec((nseq, MEM_LEN, d), lambda b, j: (b, 0, 0))
    return pl.pallas_call(
        functools.partial(_attn_kernel, nseq=nseq, by_head=by_head),
        grid=(nb, seq // tl),
        in_specs=[blk, kv, kv] + [_const_spec(w.shape) for w in weights],
        out_specs=[blk, blk, pl.BlockSpec((1, tl, LANES), lambda b, j: (b, j, 0))],
        out_shape=[jax.ShapeDtypeStruct((nb, seq, d), F32), jax.ShapeDtypeStruct((nb, seq, d), BF16),
                   jax.ShapeDtypeStruct((nb, seq, LANES), F32)],
        compiler_params=_params(("arbitrary", "arbitrary")),
        name="attn",
    )(x, mem_k, mem_v, *weights)


def _route_kernel(lg_ref, gm_ref):
    work = lg_ref[...]
    lane = lax.broadcasted_iota(jnp.int32, work.shape, 1).astype(F32)
    vals, hots = [], []
    for _ in range(TOP_K):
        m = jnp.max(work, axis=-1, keepdims=True)
        idx = jnp.min(jnp.where(work == m, lane, float(LANES)), axis=-1, keepdims=True)
        hot = lane == idx
        vals.append(m)
        hots.append(hot)
        work = jnp.where(hot, -jnp.inf, work)
    es = [jnp.exp(v - vals[0]) for v in vals]
    tot = es[0] + es[1] + es[2] + es[3]
    gm = jnp.zeros(work.shape, F32)
    for hot, e in zip(hots, es):
        gm = jnp.where(hot, e / tot, gm)
    gm_ref[...] = gm


def _route(logits, tr=512):
    t = logits.shape[0]
    blk = pl.BlockSpec((tr, LANES), lambda i: (i, 0))
    return pl.pallas_call(
        _route_kernel, grid=(t // tr,), in_specs=[blk], out_specs=blk,
        out_shape=jax.ShapeDtypeStruct((t, LANES), F32),
        compiler_params=_params(("arbitrary",)), name="route",
    )(logits)


MOE_TB = 1536
MOE_CAP = 224


def _moe_kernel(xn_ref, gm_ref, w1_ref, b1_ref, w2_ref, b2_ref, y_ref, g_s, c_s, gt_s, ct_s):
    e = pl.program_id(1)
    tb, cap = MOE_TB, MOE_CAP

    @pl.when(e == 0)
    def _():
        gm = gm_ref[...]
        hot = (gm > 0.0).astype(BF16)
        r = lax.broadcasted_iota(jnp.int32, (tb, tb), 0)
        cidx = lax.broadcasted_iota(jnp.int32, (tb, tb), 1)
        low = (cidx < r).astype(BF16)
        g_s[...] = gm
        c_s[...] = _dot(low, hot)
        gt = gm.T
        gt_s[...] = gt
        up = (r < cidx).astype(BF16)
        ct_s[...] = _dot((gt > 0.0).astype(BF16), up)
        y_ref[...] = jnp.zeros_like(y_ref)

    lane = lax.broadcasted_iota(jnp.int32, (tb, LANES), 1)
    pick = lane == e
    g_col = jnp.sum(jnp.where(pick, g_s[...], 0.0), axis=-1, keepdims=True)
    c_col = jnp.sum(jnp.where(pick, c_s[...], 0.0), axis=-1, keepdims=True)
    g_row = gt_s[pl.ds(e, 1), :]
    c_row = ct_s[pl.ds(e, 1), :]
    n = jnp.sum((g_row > 0.0).astype(F32)).astype(jnp.int32)
    sub_c = lax.broadcasted_iota(jnp.int32, (cap, 1), 0).astype(F32)
    lane_c = lax.broadcasted_iota(jnp.int32, (1, cap), 1).astype(F32)

    def one_pass(s, carry):
        base = (s * cap).astype(F32)
        hit = (g_row > 0.0) & (c_row - base == sub_c)
        p = jnp.where(hit, 1.0, 0.0).astype(BF16)
        g_rows = jnp.sum(jnp.where(hit, g_row, 0.0), axis=-1, keepdims=True)
        xin = _dot(p, xn_ref[...]).astype(BF16)
        hh = _dot(xin, w1_ref[0]) + b1_ref[0]
        glu = jnp.minimum(hh[:, :D_FF], SWIGLU_LIMIT)
        lin = jnp.clip(hh[:, D_FF:], -SWIGLU_LIMIT, SWIGLU_LIMIT)
        act = glu * _sigmoid(SWIGLU_ALPHA * glu) * (lin + 1.0)
        out = (_dot(act.astype(BF16), w2_ref[0]) + b2_ref[0]) * g_rows
        hit_t = (g_col > 0.0) & (c_col - base == lane_c)
        pt = jnp.where(hit_t, 1.0, 0.0).astype(BF16)
        y_ref[...] += _dot(pt, out.astype(BF16))
        return carry

    lax.fori_loop(0, (n + cap - 1) // cap, one_pass, 0)


def _moe(xn, gm, w1_bf, b1, w2_bf, b2):
    t, d = xn.shape
    tb = MOE_TB
    return pl.pallas_call(
        _moe_kernel,
        grid=(t // tb, N_EXPERTS),
        in_specs=[pl.BlockSpec((tb, d), lambda i, e: (i, 0)),
                  pl.BlockSpec((tb, LANES), lambda i, e: (i, 0)),
                  pl.BlockSpec((1, d, 2 * D_FF), lambda i, e: (e, 0, 0)),
                  pl.BlockSpec((1, 1, 2 * D_FF), lambda i, e: (e, 0, 0)),
                  pl.BlockSpec((1, D_FF, d), lambda i, e: (e, 0, 0)),
                  pl.BlockSpec((1, 1, d), lambda i, e: (e, 0, 0))],
        out_specs=pl.BlockSpec((tb, d), lambda i, e: (i, 0)),
        out_shape=jax.ShapeDtypeStruct((t, d), F32),
        scratch_shapes=[pltpu.VMEM((tb, LANES), F32), pltpu.VMEM((tb, LANES), F32),
                        pltpu.VMEM((LANES, tb), F32), pltpu.VMEM((LANES, tb), F32)],
        compiler_params=_params(("arbitrary", "arbitrary")),
        name="moe",
    )(xn, gm, w1_bf, b1, w2_bf, b2)


def _final_kernel(x_ref, y_ref, g_ref, o_ref):
    o_ref[...] = _rms(x_ref[...] + y_ref[...], g_ref[...])


def _final(x2, y_all, row0, g, tf=512):
    t, d = x2.shape
    off = row0 // tf
    return pl.pallas_call(
        _final_kernel, grid=(t // tf,),
        in_specs=[pl.BlockSpec((tf, d), lambda i: (i, 0)), pl.BlockSpec((tf, d), lambda i: (i + off, 0)),
                  _const_spec((1, d))],
        out_specs=pl.BlockSpec((tf, d), lambda i: (i, 0)),
        out_shape=jax.ShapeDtypeStruct((t, d), F32),
        compiler_params=_params(("arbitrary",)), name="final",
    )(x2, y_all, g)


SAMPLE_PAD = HGRN_CHUNK
STEP_ROWS = 256
ATTN_SEQS = 2


def kernel(x_prompt, x_sample, mem_prompt, state_hgrn, cache_mem_k, cache_mem_v, norm_mix_g, w_in, hgrn_lb_logits,
           hgrn_norm_g, gmlp_ln_g, gmlp_ln_b, gmlp_w_s, gmlp_b_s, w_out, norm_x_g, norm_mem_g, w_xq, w_xk, w_xv, w_xo,
           norm_ffn_g, w_router, b_router, w1, b1, w2, b2, norm_final_g):
    assert w_in.shape[0] == 1, "one layer"
    nbp, seq, d = x_prompt.shape
    nbs, dseq, _ = x_sample.shape
    mw = _mixer_weights(norm_mix_g[0], w_in[0], hgrn_lb_logits, hgrn_norm_g[0], gmlp_ln_g[0], gmlp_ln_b[0],
                        gmlp_w_s[0], gmlp_b_s[0], w_out[0])
    aw = _attn_weights(norm_x_g[0], w_xq[0], w_xo[0], norm_ffn_g[0], w_router[0], b_router[0])

    mk, mv, mkb, mvb = _memkv(mem_prompt, norm_mem_g[0].reshape(1, d), w_xk[0].astype(BF16), w_xv[0].astype(BF16))
    s0p = jnp.zeros((nbp, A_HEADS, A_DK, A_DK), F32)
    x1p, s_p, _ = _mixer(x_prompt, s0p, mw, tl=STEP_ROWS, nvalid=HGRN_CHUNK, per_chunk=False)
    x2p, hnp, lgp = _attn(x1p, mkb, mvb, aw, tl=STEP_ROWS, nseq=1)

    spb = STEP_ROWS // SAMPLE_PAD
    xs = jnp.pad(x_sample, ((0, 0), (0, SAMPLE_PAD - dseq), (0, 0))).reshape(nbs // spb, STEP_ROWS, d)
    x1s, s_s, vns = _mixer(xs, state_hgrn[0], mw, tl=STEP_ROWS, nvalid=dseq, per_chunk=True)
    x2s, hns, lgs = _attn(x1s.reshape(nbs // ATTN_SEQS, ATTN_SEQS * SAMPLE_PAD, d), cache_mem_k, cache_mem_v, aw,
                          tl=ATTN_SEQS * SAMPLE_PAD, nseq=ATTN_SEQS)

    def real(a):
        return a.reshape(nbs, SAMPLE_PAD, a.shape[-1])[:, :dseq].reshape(nbs * dseq, a.shape[-1])

    tp = nbp * seq

    xn = jnp.concatenate([hnp.reshape(tp, d), real(hns)], axis=0)
    gm = _route(jnp.concatenate([lgp.reshape(tp, LANES), real(lgs)], axis=0))
    y_all = _moe(xn, gm, w1[0].astype(BF16), b1[0].reshape(N_EXPERTS, 1, 2 * D_FF), w2[0].astype(BF16),
                 b2[0].reshape(N_EXPERTS, 1, d))
    gfin = norm_final_g.reshape(1, d)
    y_prompt = _final(x2p.reshape(tp, d), y_all, 0, gfin).reshape(nbp, seq, d)
    y_sample = _final(real(x2s), y_all, tp, gfin).reshape(nbs, dseq, d)

    vn_s = real(vns).reshape(1, nbs, dseq, B_GROUPS, B_CH)
    return (y_prompt, y_sample, s_p[None], mk.reshape(1, nbp, MEM_LEN, X_HEADS, X_HD),
            mv.reshape(1, nbp, MEM_LEN, X_HEADS, X_HD), s_s[None], vn_s)
```

```python
import functools
import math

import jax
import jax.numpy as jnp
import numpy as np
from jax import lax
from jax.experimental import pallas as pl
from jax.experimental.pallas import tpu as pltpu

F32 = jnp.float32
BF16 = jnp.bfloat16

D_MODEL = 1024
A_HEADS = 4
A_DK = 128
A_WIDTH = 512
B_GROUPS = 4
B_CH = 128
B_WIDTH = 512
IN_COLS = 4 * A_WIDTH + 2 * B_WIDTH
HGRN_CHUNK = 64
GMLP_CHUNK = 128
MEM_LEN = 256
X_HEADS = 4
X_HD = 256
N_EXPERTS = 32
TOP_K = 4
D_FF = 1024
SWIGLU_LIMIT = 7.0
SWIGLU_ALPHA = 1.702
EPS = 1e-6

LANES = 128
VMEM_LIMIT = 56 * 1024 * 1024

NT = (((1,), (1,)), ((), ()))
TN = (((0,), (0,)), ((), ()))


def _dot(a, b):
    return jnp.dot(a, b, preferred_element_type=F32)


def _dot_nt(a, b):
    return lax.dot_general(a, b, NT, preferred_element_type=F32)


def _dot_tn(a, b):
    return lax.dot_general(a, b, TN, preferred_element_type=F32)


def _sigmoid(x):
    return 1.0 / (1.0 + jnp.exp(-x))


def _silu(x):
    return x * _sigmoid(x)


def _gelu(x):
    c = math.sqrt(2.0 / math.pi)
    return x * (0.5 * (1.0 + jnp.tanh(c * (x + 0.044715 * (x * x * x)))))


def _rms(x, g):
    return x * lax.rsqrt(jnp.mean(x * x, axis=-1, keepdims=True) + EPS) * g


def _split_bf16(x):
    hi = x.astype(BF16)
    lo = (x - hi.astype(F32)).astype(BF16)
    return hi, lo


def _params(sem):
    return pltpu.CompilerParams(dimension_semantics=sem, vmem_limit_bytes=VMEM_LIMIT)


def _const_spec(shape):
    nd = len(shape)
    return pl.BlockSpec(shape, lambda *_: (0,) * nd)


def _memkv_kernel(mem_ref, g_ref, wk_ref, wv_ref, k_ref, v_ref, kb_ref, vb_ref):
    mn = _rms(mem_ref[0], g_ref[...]).astype(BF16)
    k = _dot(mn, wk_ref[...])
    v = _dot(mn, wv_ref[...])
    k_ref[0] = k
    v_ref[0] = v
    kb_ref[0] = k.astype(BF16)
    vb_ref[0] = v.astype(BF16)


def _memkv(mem, g, wk_bf, wv_bf):
    nb, m, d = mem.shape
    blk = pl.BlockSpec((1, m, d), lambda b: (b, 0, 0))
    return pl.pallas_call(
        _memkv_kernel,
        grid=(nb,),
        in_specs=[blk, _const_spec((1, d)), _const_spec((d, d)), _const_spec((d, d))],
        out_specs=[blk, blk, blk, blk],
        out_shape=[jax.ShapeDtypeStruct((nb, m, d), F32)] * 2 + [jax.ShapeDtypeStruct((nb, m, d), BF16)] * 2,
        compiler_params=_params(("arbitrary",)),
        name="memkv",
    )(mem, g, wk_bf, wv_bf)


_LEVELS = (32, 16, 8)
_DIAG = 8


def _hgrn_consts():
    c = HGRN_CHUNK
    t = np.arange(c)
    rows = [(t[None, :] <= t[:, None]).astype(np.float32)]
    masks = []
    for h in _LEVELS:
        m = np.zeros((c, c), np.float32)
        for r in range(c):
            ref = (r // (2 * h)) * 2 * h + h - 1
            if r % (2 * h) >= h:
                m[r, ref + 1:r + 1] = 1.0
            else:
                m[r, r + 1:ref + 1] = 1.0
        rows.append(m)
        tt, ss = t[:, None], t[None, :]
        masks.append(((tt // (2 * h) == ss // (2 * h)) & (tt % (2 * h) >= h) & (ss % (2 * h) < h)).astype(np.float32))
    rows.append((t[None, :] > t[:, None]).astype(np.float32))
    return np.concatenate(rows, 0), np.stack(masks)


def _lower_bound(lbl):
    mx = jnp.max(lbl, axis=0, keepdims=True)
    e = jnp.exp(lbl - mx)
    return e[0:1] / jnp.sum(e, axis=0, keepdims=True)


def _hgrn_gates(p, lb):
    q = _silu(p[:, 0:A_WIDTH])
    fg = lb + (1.0 - lb) * _sigmoid(p[:, A_WIDTH:2 * A_WIDTH])
    return q, 1.0 - fg, jnp.log(fg), p[:, 2 * A_WIDTH:3 * A_WIDTH]


def _head_norm_gate(o, hng, g):
    outs = []
    for hd in range(A_HEADS):
        sl = slice(hd * A_DK, (hd + 1) * A_DK)
        oh = o[:, sl]
        outs.append(oh * lax.rsqrt(jnp.mean(oh * oh, axis=-1, keepdims=True) + EPS) * hng[:, sl])
    return jnp.concatenate(outs, axis=-1) * _silu(g)


def _gmlp_norm(v, lng, lnb):
    vf = _gelu(v)
    outs = []
    for gi in range(B_GROUPS):
        sl = slice(gi * B_CH, (gi + 1) * B_CH)
        vg = vf[:, sl]
        mu = jnp.mean(vg, axis=-1, keepdims=True)
        dv = vg - mu
        var = jnp.mean(dv * dv, axis=-1, keepdims=True)
        outs.append(dv * lax.rsqrt(var + EPS) * lng[:, sl] + lnb[:, sl])
    return jnp.concatenate(outs, axis=-1)


def _mixer_kernel(x_ref, s0_ref, gmix_ref, win_ref, lbl_ref, mst_ref, msk_ref, hng_ref, lng_ref, lnb_ref,
                  ws_ref, bsb_ref, wout_ref, x1_ref, sfin_ref, vn_ref,
                  st_ref, q_s, k_s, v_s, lf_s, oa_s, *, tl, nvalid, per_chunk, gchunk):
    j = pl.program_id(1)
    c = HGRN_CHUNK

    if not per_chunk:
        @pl.when(j == 0)
        def _():
            for hd in range(A_HEADS):
                st_ref[hd] = s0_ref[0, hd].T

    x = x_ref[0]
    h = _rms(x, gmix_ref[...]).astype(BF16)
    p = _dot(h, win_ref[...])
    lb = _lower_bound(lbl_ref[...])
    q, k, lf, v = _hgrn_gates(p, lb)
    if nvalid < c:
        row = lax.broadcasted_iota(jnp.int32, (tl, 1), 0)
        lf = jnp.where(row % c < nvalid, lf, 0.0)
    q_s[...] = q
    k_s[...] = k
    lf_s[...] = lf
    v_s[...] = v

    sub = lax.broadcasted_iota(jnp.int32, (c // _DIAG, _DIAG, 1), 1)

    def chunk(ci, carry):
        r0 = pl.multiple_of(ci * c, c)
        lfc = lf_s[pl.ds(r0, c), :]
        hi, lo = _split_bf16(lfc)
        mst = mst_ref[...]
        bexp = _dot(mst, hi) + _dot(mst, lo)
        e = jnp.exp(bexp)
        qc = q_s[pl.ds(r0, c), :]
        kc = k_s[pl.ds(r0, c), :]
        vc = v_s[pl.ds(r0, c), :]
        for hd in range(A_HEADS):
            sl = slice(hd * A_DK, (hd + 1) * A_DK)
            qh, kh, vh, eh = qc[:, sl], kc[:, sl], vc[:, sl], e[:, sl]
            vb = vh.astype(BF16)
            st = s0_ref[ci, hd].T if per_chunk else st_ref[hd]
            sc = jnp.zeros((c, c), F32)
            for li in range(len(_LEVELS)):
                el = eh[(li + 1) * c:(li + 2) * c]
                sc = sc + _dot_nt((qh * el).astype(BF16), (kh * el).astype(BF16)) * msk_ref[li]
            o = _dot(sc.astype(BF16), vb) + _dot_nt((qh * eh[0:c]).astype(BF16), st.astype(BF16))
            nblk = c // _DIAG
            b3 = bexp[0:c, sl].reshape(nblk, _DIAG, A_DK)
            q3 = qh.reshape(nblk, _DIAG, A_DK)
            k3 = kh.reshape(nblk, _DIAG, A_DK)
            v3 = vh.reshape(nblk, _DIAG, A_DK)
            od = jnp.zeros((nblk, _DIAG, A_DK), F32)
            for s in range(_DIAG):
                dec = jnp.exp(jnp.minimum(b3 - b3[:, s:s + 1, :], 0.0))
                w = jnp.sum(q3 * dec * k3[:, s:s + 1, :], axis=-1, keepdims=True)
                w = jnp.where(sub >= s, w, 0.0)
                od = od + w * v3[:, s:s + 1, :]
            oa_s[pl.ds(r0, c), sl] = o + od.reshape(c, A_DK)
            kd = (kh * eh[4 * c:5 * c]).astype(BF16)
            st_new = st * eh[c - 1:c] + _dot_tn(vb, kd)
            if per_chunk:
                sfin_ref[ci, hd] = st_new.T
            else:
                st_ref[hd] = st_new
        return carry

    lax.fori_loop(0, tl // c, chunk, 0, unroll=True)

    o_a = _head_norm_gate(oa_s[...], hng_ref[...], p[:, 3 * A_WIDTH:4 * A_WIDTH])
    ug = _gelu(p[:, 4 * A_WIDTH:4 * A_WIDTH + B_WIDTH])
    vn = _gmlp_norm(p[:, 4 * A_WIDTH + B_WIDTH:], lng_ref[...], lnb_ref[...])
    vn_ref[0] = vn
    sg = []
    for gi in range(B_GROUPS):
        sl = slice(gi * B_CH, (gi + 1) * B_CH)
        wsg = ws_ref[gi][0:gchunk, 0:gchunk]
        bsg = bsb_ref[gi][0:gchunk, :]
        rows = []
        for cc in range(tl // gchunk):
            vcc = vn[cc * gchunk:(cc + 1) * gchunk, sl].astype(BF16)
            rows.append(_dot(wsg, vcc) + bsg)
        sg.append(jnp.concatenate(rows, axis=0))
    o_b = ug * jnp.concatenate(sg, axis=-1)
    y = _dot(o_a.astype(BF16), wout_ref[0:A_WIDTH, :]) + _dot(o_b.astype(BF16), wout_ref[A_WIDTH:, :])
    x1_ref[0] = x + y

    if not per_chunk:
        @pl.when(j == pl.num_programs(1) - 1)
        def _():
            for hd in range(A_HEADS):
                sfin_ref[0, hd] = st_ref[hd].T


def _mixer_weights(norm_mix_g, w_in, lb_logits, hgrn_norm_g, ln_g, ln_b, w_s, b_s, w_out):
    mst, msk = _hgrn_consts()
    ws_bf = jnp.tril(w_s).astype(BF16)
    bsb = jnp.broadcast_to(b_s[:, :, None], (B_GROUPS, GMLP_CHUNK, B_CH))
    return (norm_mix_g.reshape(1, D_MODEL), w_in.astype(BF16), lb_logits, jnp.asarray(mst, BF16), jnp.asarray(msk, F32),
            hgrn_norm_g.reshape(1, A_WIDTH), ln_g.reshape(1, B_WIDTH), ln_b.reshape(1, B_WIDTH), ws_bf, bsb,
            w_out.astype(BF16))


def _mixer(x, s0, weights, *, tl, nvalid, per_chunk):
    nb, seq, d = x.shape
    c = HGRN_CHUNK
    spb = tl // c if per_chunk else 1
    gchunk = c if per_chunk else GMLP_CHUNK
    blk = pl.BlockSpec((1, tl, d), lambda b, j: (b, j, 0))
    sblk = pl.BlockSpec((spb, A_HEADS, A_DK, A_DK), lambda b, j: (b, 0, 0, 0))
    act = pltpu.VMEM((tl, A_WIDTH), F32)
    return pl.pallas_call(
        functools.partial(_mixer_kernel, tl=tl, nvalid=nvalid, per_chunk=per_chunk, gchunk=gchunk),
        grid=(nb, seq // tl),
        in_specs=[blk, sblk] + [_const_spec(w.shape) for w in weights],
        out_specs=[blk, sblk, pl.BlockSpec((1, tl, B_WIDTH), lambda b, j: (b, j, 0))],
        out_shape=[jax.ShapeDtypeStruct((nb, seq, d), F32),
                   jax.ShapeDtypeStruct(s0.shape, F32),
                   jax.ShapeDtypeStruct((nb, seq, B_WIDTH), F32)],
        scratch_shapes=[pltpu.VMEM((A_HEADS, A_DK, A_DK), F32), act, act, act, act, act],
        compiler_params=_params(("arbitrary", "arbitrary")),
        name="mixer",
    )(x, s0, *weights)


def _attend(q, k, v):
    s = _dot_nt(q.astype(BF16), k.astype(BF16)) * (X_HD ** -0.5)
    pexp = jnp.exp(s - jnp.max(s, axis=-1, keepdims=True))
    pr = pexp / jnp.sum(pexp, axis=-1, keepdims=True)
    return _dot(pr.astype(BF16), v.astype(BF16))


def _attn_tail(x2, gf_ref, wr_ref, br_ref, x2_ref, hn_ref, lg_ref):
    x2_ref[0] = x2
    hn = _rms(x2, gf_ref[...]).astype(BF16)
    hn_ref[0] = hn
    lg_ref[0] = _dot(hn, wr_ref[...]) + br_ref[...]


def _attn_kernel(x_ref, k_ref, v_ref, gx_ref, wq_ref, wo_ref, gf_ref, wr_ref, br_ref, x2_ref, hn_ref, lg_ref):
    x = x_ref[0]
    h = _rms(x, gx_ref[...]).astype(BF16)
    q = _dot(h, wq_ref[...])
    outs = []
    for hd in range(X_HEADS):
        sl = slice(hd * X_HD, (hd + 1) * X_HD)
        outs.append(_attend(q[:, sl], k_ref[0, :, sl], v_ref[0, :, sl]))
    o = jnp.concatenate(outs, axis=-1).astype(BF16)
    _attn_tail(x + _dot(o, wo_ref[...]), gf_ref, wr_ref, br_ref, x2_ref, hn_ref, lg_ref)


def _attn_by_head_kernel(x_ref, k_ref, v_ref, gx_ref, wq_ref, wo_ref, gf_ref, wr_ref, br_ref, x2_ref, hn_ref, lg_ref,
                         *, nseq):
    x = x_ref[0]
    rows = x.shape[0] // nseq
    h = _rms(x, gx_ref[...]).astype(BF16)
    q = _dot(h, wq_ref[...])
    per_seq = []
    for g in range(nseq):
        qg = q[g * rows:(g + 1) * rows]
        outs = [_attend(qg[:, hd * X_HD:(hd + 1) * X_HD], k_ref[g, :, hd, :], v_ref[g, :, hd, :])
                for hd in range(X_HEADS)]
        per_seq.append(jnp.concatenate(outs, axis=-1))
    o = jnp.concatenate(per_seq, axis=0).astype(BF16)
    _attn_tail(x + _dot(o, wo_ref[...]), gf_ref, wr_ref, br_ref, x2_ref, hn_ref, lg_ref)


def _attn_weights(norm_x_g, w_xq, w_xo, norm_ffn_g, w_router, b_router):
    wr = jnp.zeros((D_MODEL, LANES), BF16).at[:, :N_EXPERTS].set(w_router.astype(BF16))
    br = jnp.full((1, LANES), -1e30, F32).at[0, :N_EXPERTS].set(b_router)
    return (norm_x_g.reshape(1, D_MODEL), w_xq.astype(BF16), w_xo.astype(BF16), norm_ffn_g.reshape(1, D_MODEL), wr, br)


def _attn_out(nb, seq, d):
    return [jax.ShapeDtypeStruct((nb, seq, d), F32), jax.ShapeDtypeStruct((nb, seq, d), BF16),
            jax.ShapeDtypeStruct((nb, seq, LANES), F32)]


def _attn(x, mem_k, mem_v, weights, *, tl):
    nb, seq, d = x.shape
    blk = pl.BlockSpec((1, tl, d), lambda b, j: (b, j, 0))
    kv = pl.BlockSpec((1, MEM_LEN, d), lambda b, j: (b, 0, 0))
    return pl.pallas_call(
        _attn_kernel,
        grid=(nb, seq // tl),
        in_specs=[blk, kv, kv] + [_const_spec(w.shape) for w in weights],
        out_specs=[blk, blk, pl.BlockSpec((1, tl, LANES), lambda b, j: (b, j, 0))],
        out_shape=_attn_out(nb, seq, d),
        compiler_params=_params(("arbitrary", "arbitrary")),
        name="attn",
    )(x, mem_k, mem_v, *weights)


def _attn_by_head(x, mem_k, mem_v, weights, *, nseq):
    nb, rows, d = x.shape
    blk = pl.BlockSpec((1, rows, d), lambda b: (b, 0, 0))
    kv = pl.BlockSpec((None, nseq, MEM_LEN, X_HEADS, X_HD), lambda b: (0, b, 0, 0, 0))
    return pl.pallas_call(
        functools.partial(_attn_by_head_kernel, nseq=nseq),
        grid=(nb,),
        in_specs=[blk, kv, kv] + [_const_spec(w.shape) for w in weights],
        out_specs=[blk, blk, pl.BlockSpec((1, rows, LANES), lambda b: (b, 0, 0))],
        out_shape=_attn_out(nb, rows, d),
        compiler_params=_params(("arbitrary",)),
        name="attn_by_head",
    )(x, mem_k, mem_v, *weights)


def _route_kernel(lg_ref, gm_ref):
    work = lg_ref[...]
    lane = lax.broadcasted_iota(jnp.int32, work.shape, 1).astype(F32)
    vals, hots = [], []
    for _ in range(TOP_K):
        m = jnp.max(work, axis=-1, keepdims=True)
        idx = jnp.min(jnp.where(work == m, lane, float(LANES)), axis=-1, keepdims=True)
        hot = lane == idx
        vals.append(m)
        hots.append(hot)
        work = jnp.where(hot, -jnp.inf, work)
    es = [jnp.exp(v - vals[0]) for v in vals]
    tot = es[0] + es[1] + es[2] + es[3]
    gm = jnp.zeros(work.shape, F32)
    for hot, e in zip(hots, es):
        gm = jnp.where(hot, e / tot, gm)
    gm_ref[...] = gm


def _route(logits, tr=512):
    t = logits.shape[0]
    blk = pl.BlockSpec((tr, LANES), lambda i: (i, 0))
    return pl.pallas_call(
        _route_kernel, grid=(t // tr,), in_specs=[blk], out_specs=blk,
        out_shape=jax.ShapeDtypeStruct((t, LANES), F32),
        compiler_params=_params(("arbitrary",)), name="route",
    )(logits)


MOE_TB = 1536
MOE_CAP = 224


def _moe_kernel(xn_ref, gm_ref, w1_ref, b1_ref, w2_ref, b2_ref, y_ref, g_s, c_s, gt_s, ct_s):
    e = pl.program_id(1)
    tb, cap = MOE_TB, MOE_CAP

    @pl.when(e == 0)
    def _():
        gm = gm_ref[...]
        hot = (gm > 0.0).astype(BF16)
        r = lax.broadcasted_iota(jnp.int32, (tb, tb), 0)
        cidx = lax.broadcasted_iota(jnp.int32, (tb, tb), 1)
        low = (cidx < r).astype(BF16)
        g_s[...] = gm
        c_s[...] = _dot(low, hot)
        gt = gm.T
        gt_s[...] = gt
        up = (r < cidx).astype(BF16)
        ct_s[...] = _dot((gt > 0.0).astype(BF16), up)
        y_ref[...] = jnp.zeros_like(y_ref)

    g_row = gt_s[pl.ds(e, 1), :]
    c_row = ct_s[pl.ds(e, 1), :]
    n = jnp.sum((g_row > 0.0).astype(F32)).astype(jnp.int32)
    sub_c = lax.broadcasted_iota(jnp.int32, (cap, 1), 0).astype(F32)
    lane_c = lax.broadcasted_iota(jnp.int32, (1, cap), 1).astype(F32)

    def one_pass(s, carry):
        pick = lax.broadcasted_iota(jnp.int32, (tb, LANES), 1) == e
        g_col = jnp.sum(jnp.where(pick, g_s[...], 0.0), axis=-1, keepdims=True)
        c_col = jnp.sum(jnp.where(pick, c_s[...], 0.0), axis=-1, keepdims=True)
        base = (s * cap).astype(F32)
        hit = (g_row > 0.0) & (c_row - base == sub_c)
        p = jnp.where(hit, 1.0, 0.0).astype(BF16)
        g_rows = jnp.sum(jnp.where(hit, g_row, 0.0), axis=-1, keepdims=True)
        xin = _dot(p, xn_ref[...]).astype(BF16)
        hh = _dot(xin, w1_ref[0]) + b1_ref[0]
        glu = jnp.minimum(hh[:, :D_FF], SWIGLU_LIMIT)
        lin = jnp.clip(hh[:, D_FF:], -SWIGLU_LIMIT, SWIGLU_LIMIT)
        act = glu * _sigmoid(SWIGLU_ALPHA * glu) * (lin + 1.0)
        out = (_dot(act.astype(BF16), w2_ref[0]) + b2_ref[0]) * g_rows
        hit_t = (g_col > 0.0) & (c_col - base == lane_c)
        pt = jnp.where(hit_t, 1.0, 0.0).astype(BF16)
        y_ref[...] += _dot(pt, out.astype(BF16))
        return carry

    lax.fori_loop(0, (n + cap - 1) // cap, one_pass, 0)


def _moe(xn, gm, w1_bf, b1, w2_bf, b2):
    t, d = xn.shape
    tb = MOE_TB
    return pl.pallas_call(
        _moe_kernel,
        grid=(t // tb, N_EXPERTS),
        in_specs=[pl.BlockSpec((tb, d), lambda i, e: (i, 0)),
                  pl.BlockSpec((tb, LANES), lambda i, e: (i, 0)),
                  pl.BlockSpec((1, d, 2 * D_FF), lambda i, e: (e, 0, 0)),
                  pl.BlockSpec((1, 1, 2 * D_FF), lambda i, e: (e, 0, 0)),
                  pl.BlockSpec((1, D_FF, d), lambda i, e: (e, 0, 0)),
                  pl.BlockSpec((1, 1, d), lambda i, e: (e, 0, 0))],
        out_specs=pl.BlockSpec((tb, d), lambda i, e: (i, 0)),
        out_shape=jax.ShapeDtypeStruct((t, d), F32),
        scratch_shapes=[pltpu.VMEM((tb, LANES), F32), pltpu.VMEM((tb, LANES), F32),
                        pltpu.VMEM((LANES, tb), F32), pltpu.VMEM((LANES, tb), F32)],
        compiler_params=_params(("arbitrary", "arbitrary")),
        name="moe",
    )(xn, gm, w1_bf, b1, w2_bf, b2)


def _final_kernel(x_ref, y_ref, g_ref, o_ref):
    o_ref[...] = _rms(x_ref[...] + y_ref[...], g_ref[...])


def _final(x2, y_all, row0, g, tf=512):
    t, d = x2.shape
    off = row0 // tf
    return pl.pallas_call(
        _final_kernel, grid=(t // tf,),
        in_specs=[pl.BlockSpec((tf, d), lambda i: (i, 0)), pl.BlockSpec((tf, d), lambda i: (i + off, 0)),
                  _const_spec((1, d))],
        out_specs=pl.BlockSpec((tf, d), lambda i: (i, 0)),
        out_shape=jax.ShapeDtypeStruct((t, d), F32),
        compiler_params=_params(("arbitrary",)), name="final",
    )(x2, y_all, g)


SAMPLE_PAD = HGRN_CHUNK
STEP_ROWS = 256
ATTN_SEQS = 2


def kernel(x_prompt, x_sample, mem_prompt, state_hgrn, cache_mem_k, cache_mem_v, norm_mix_g, w_in, hgrn_lb_logits,
           hgrn_norm_g, gmlp_ln_g, gmlp_ln_b, gmlp_w_s, gmlp_b_s, w_out, norm_x_g, norm_mem_g, w_xq, w_xk, w_xv, w_xo,
           norm_ffn_g, w_router, b_router, w1, b1, w2, b2, norm_final_g):
    assert w_in.shape[0] == 1, "one layer"
    nbp, seq, d = x_prompt.shape
    nbs, dseq, _ = x_sample.shape
    mw = _mixer_weights(norm_mix_g[0], w_in[0], hgrn_lb_logits, hgrn_norm_g[0], gmlp_ln_g[0], gmlp_ln_b[0],
                        gmlp_w_s[0], gmlp_b_s[0], w_out[0])
    aw = _attn_weights(norm_x_g[0], w_xq[0], w_xo[0], norm_ffn_g[0], w_router[0], b_router[0])

    mk, mv, mkb, mvb = _memkv(mem_prompt, norm_mem_g[0].reshape(1, d), w_xk[0].astype(BF16), w_xv[0].astype(BF16))
    s0p = jnp.zeros((nbp, A_HEADS, A_DK, A_DK), F32)
    x1p, s_p, _ = _mixer(x_prompt, s0p, mw, tl=STEP_ROWS, nvalid=HGRN_CHUNK, per_chunk=False)
    x2p, hnp, lgp = _attn(x1p, mkb, mvb, aw, tl=STEP_ROWS)

    spb = STEP_ROWS // SAMPLE_PAD
    xs = jnp.pad(x_sample, ((0, 0), (0, SAMPLE_PAD - dseq), (0, 0))).reshape(nbs // spb, STEP_ROWS, d)
    x1s, s_s, vns = _mixer(xs, state_hgrn[0], mw, tl=STEP_ROWS, nvalid=dseq, per_chunk=True)
    x2s, hns, lgs = _attn_by_head(x1s.reshape(nbs // ATTN_SEQS, ATTN_SEQS * SAMPLE_PAD, d), cache_mem_k, cache_mem_v,
                                  aw, nseq=ATTN_SEQS)

    def real(a):
        return a.reshape(nbs, SAMPLE_PAD, a.shape[-1])[:, :dseq].reshape(nbs * dseq, a.shape[-1])

    tp = nbp * seq

    xn = jnp.concatenate([hnp.reshape(tp, d), real(hns)], axis=0)
    gm = _route(jnp.concatenate([lgp.reshape(tp, LANES), real(lgs)], axis=0))
    y_all = _moe(xn, gm, w1[0].astype(BF16), b1[0].reshape(N_EXPERTS, 1, 2 * D_FF), w2[0].astype(BF16),
                 b2[0].reshape(N_EXPERTS, 1, d))
    gfin = norm_final_g.reshape(1, d)
    y_prompt = _final(x2p.reshape(tp, d), y_all, 0, gfin).reshape(nbp, seq, d)
    y_sample = _final(real(x2s), y_all, tp, gfin).reshape(nbs, dseq, d)

    vn_s = real(vns).reshape(1, nbs, dseq, B_GROUPS, B_CH)
    return (y_prompt, y_sample, s_p[None], mk.reshape(1, nbp, MEM_LEN, X_HEADS, X_HD),
            mv.reshape(1, nbp, MEM_LEN, X_HEADS, X_HD), s_s[None], vn_s)
```

```python
import functools
import math

import jax
import jax.numpy as jnp
import numpy as np
from jax import lax
from jax.experimental import pallas as pl
from jax.experimental.pallas import tpu as pltpu

F32 = jnp.float32
BF16 = jnp.bfloat16

D_MODEL = 1024
A_HEADS = 4
A_DK = 128
A_WIDTH = 512
B_GROUPS = 4
B_CH = 128
B_WIDTH = 512
IN_COLS = 4 * A_WIDTH + 2 * B_WIDTH
HGRN_CHUNK = 64
GMLP_CHUNK = 128
MEM_LEN = 256
X_HEADS = 4
X_HD = 256
N_EXPERTS = 32
TOP_K = 4
D_FF = 1024
SWIGLU_LIMIT = 7.0
SWIGLU_ALPHA = 1.702
EPS = 1e-6

LANES = 128
VMEM_LIMIT = 56 * 1024 * 1024

NT = (((1,), (1,)), ((), ()))
TN = (((0,), (0,)), ((), ()))


def _dot(a, b):
    return jnp.dot(a, b, preferred_element_type=F32)


def _dot_nt(a, b):
    return lax.dot_general(a, b, NT, preferred_element_type=F32)


def _dot_tn(a, b):
    return lax.dot_general(a, b, TN, preferred_element_type=F32)


def _sigmoid(x):
    return 1.0 / (1.0 + jnp.exp(-x))


def _silu(x):
    return x * _sigmoid(x)


def _gelu(x):
    c = math.sqrt(2.0 / math.pi)
    return x * (0.5 * (1.0 + jnp.tanh(c * (x + 0.044715 * (x * x * x)))))


def _rms(x, g):
    return x * lax.rsqrt(jnp.mean(x * x, axis=-1, keepdims=True) + EPS) * g


def _split_bf16(x):
    hi = x.astype(BF16)
    lo = (x - hi.astype(F32)).astype(BF16)
    return hi, lo


def _params(sem):
    return pltpu.CompilerParams(dimension_semantics=sem, vmem_limit_bytes=VMEM_LIMIT)


def _const_spec(shape):
    nd = len(shape)
    return pl.BlockSpec(shape, lambda *_: (0,) * nd)


def _memkv_kernel(mem_ref, g_ref, wk_ref, wv_ref, k_ref, v_ref, kb_ref, vb_ref):
    mn = _rms(mem_ref[0], g_ref[...]).astype(BF16)
    k = _dot(mn, wk_ref[...])
    v = _dot(mn, wv_ref[...])
    k_ref[0] = k
    v_ref[0] = v
    kb_ref[0] = k.astype(BF16)
    vb_ref[0] = v.astype(BF16)


def _memkv(mem, g, wk_bf, wv_bf):
    nb, m, d = mem.shape
    blk = pl.BlockSpec((1, m, d), lambda b: (b, 0, 0))
    return pl.pallas_call(
        _memkv_kernel,
        grid=(nb,),
        in_specs=[blk, _const_spec((1, d)), _const_spec((d, d)), _const_spec((d, d))],
        out_specs=[blk, blk, blk, blk],
        out_shape=[jax.ShapeDtypeStruct((nb, m, d), F32)] * 2 + [jax.ShapeDtypeStruct((nb, m, d), BF16)] * 2,
        compiler_params=_params(("arbitrary",)),
        name="memkv",
    )(mem, g, wk_bf, wv_bf)


_LEVELS = (32, 16, 8)
_DIAG = 8


def _hgrn_consts():
    c = HGRN_CHUNK
    t = np.arange(c)
    rows = [(t[None, :] <= t[:, None]).astype(np.float32)]
    masks = []
    for h in _LEVELS:
        m = np.zeros((c, c), np.float32)
        for r in range(c):
            ref = (r // (2 * h)) * 2 * h + h - 1
            if r % (2 * h) >= h:
                m[r, ref + 1:r + 1] = 1.0
            else:
                m[r, r + 1:ref + 1] = 1.0
        rows.append(m)
        tt, ss = t[:, None], t[None, :]
        masks.append(((tt // (2 * h) == ss // (2 * h)) & (tt % (2 * h) >= h) & (ss % (2 * h) < h)).astype(np.float32))
    rows.append((t[None, :] > t[:, None]).astype(np.float32))
    return np.concatenate(rows, 0), np.stack(masks)


def _lower_bound(lbl):
    mx = jnp.max(lbl, axis=0, keepdims=True)
    e = jnp.exp(lbl - mx)
    return e[0:1] / jnp.sum(e, axis=0, keepdims=True)


def _hgrn_gates(p, lb):
    q = _silu(p[:, 0:A_WIDTH])
    fg = lb + (1.0 - lb) * _sigmoid(p[:, A_WIDTH:2 * A_WIDTH])
    return q, 1.0 - fg, jnp.log(fg), p[:, 2 * A_WIDTH:3 * A_WIDTH]


def _head_norm_gate(o, hng, g):
    outs = []
    for hd in range(A_HEADS):
        sl = slice(hd * A_DK, (hd + 1) * A_DK)
        oh = o[:, sl]
        outs.append(oh * lax.rsqrt(jnp.mean(oh * oh, axis=-1, keepdims=True) + EPS) * hng[:, sl])
    return jnp.concatenate(outs, axis=-1) * _silu(g)


def _gmlp_norm(v, lng, lnb):
    vf = _gelu(v)
    outs = []
    for gi in range(B_GROUPS):
        sl = slice(gi * B_CH, (gi + 1) * B_CH)
        vg = vf[:, sl]
        mu = jnp.mean(vg, axis=-1, keepdims=True)
        dv = vg - mu
        var = jnp.mean(dv * dv, axis=-1, keepdims=True)
        outs.append(dv * lax.rsqrt(var + EPS) * lng[:, sl] + lnb[:, sl])
    return jnp.concatenate(outs, axis=-1)


def _mixer_kernel(x_ref, s0_ref, gmix_ref, win_ref, lbl_ref, mst_ref, msk_ref, hng_ref, lng_ref, lnb_ref,
                  ws_ref, bsb_ref, wout_ref, x1_ref, sfin_ref, vn_ref,
                  st_ref, q_s, k_s, v_s, lf_s, oa_s, *, tl, nvalid, per_chunk, gchunk):
    j = pl.program_id(1)
    c = HGRN_CHUNK

    if not per_chunk:
        @pl.when(j == 0)
        def _():
            for hd in range(A_HEADS):
                st_ref[hd] = s0_ref[0, hd].T

    x = x_ref[0]
    h = _rms(x, gmix_ref[...]).astype(BF16)
    p = _dot(h, win_ref[...])
    lb = _lower_bound(lbl_ref[...])
    q, k, lf, v = _hgrn_gates(p, lb)
    if nvalid < c:
        row = lax.broadcasted_iota(jnp.int32, (tl, 1), 0)
        lf = jnp.where(row % c < nvalid, lf, 0.0)
    q_s[...] = q
    k_s[...] = k
    lf_s[...] = lf
    v_s[...] = v

    sub = lax.broadcasted_iota(jnp.int32, (c // _DIAG, _DIAG, 1), 1)

    def chunk(ci, carry):
        r0 = pl.multiple_of(ci * c, c)
        lfc = lf_s[pl.ds(r0, c), :]
        hi, lo = _split_bf16(lfc)
        mst = mst_ref[...]
        short = nvalid <= _DIAG
        nlev = 0 if short else len(_LEVELS)
        if short:
            mst = jnp.concatenate([mst[0:c], mst[4 * c:5 * c]], axis=0)
        bexp = _dot(mst, hi) + _dot(mst, lo)
        e = jnp.exp(bexp)
        qc = q_s[pl.ds(r0, c), :]
        kc = k_s[pl.ds(r0, c), :]
        vc = v_s[pl.ds(r0, c), :]
        for hd in range(A_HEADS):
            sl = slice(hd * A_DK, (hd + 1) * A_DK)
            qh, kh, vh, eh = qc[:, sl], kc[:, sl], vc[:, sl], e[:, sl]
            vb = vh.astype(BF16)
            st = s0_ref[ci, hd].T if per_chunk else st_ref[hd]
            o = jnp.zeros((c, A_DK), F32)
            if nlev:
                sc = jnp.zeros((c, c), F32)
                for li in range(nlev):
                    el = eh[(li + 1) * c:(li + 2) * c]
                    sc = sc + _dot_nt((qh * el).astype(BF16), (kh * el).astype(BF16)) * msk_ref[li]
                o = _dot(sc.astype(BF16), vb)
            o = o + _dot_nt((qh * eh[0:c]).astype(BF16), st.astype(BF16))
            nblk = 1 if short else c // _DIAG
            nrow = nblk * _DIAG
            b3 = bexp[0:nrow, sl].reshape(nblk, _DIAG, A_DK)
            q3 = qh[0:nrow].reshape(nblk, _DIAG, A_DK)
            k3 = kh[0:nrow].reshape(nblk, _DIAG, A_DK)
            v3 = vh[0:nrow].reshape(nblk, _DIAG, A_DK)
            od = jnp.zeros((nblk, _DIAG, A_DK), F32)
            for s in range(min(nvalid, _DIAG)):
                dec = jnp.exp(jnp.minimum(b3 - b3[:, s:s + 1, :], 0.0))
                w = jnp.sum(q3 * dec * k3[:, s:s + 1, :], axis=-1, keepdims=True)
                w = jnp.where(sub[0:nblk] >= s, w, 0.0)
                od = od + w * v3[:, s:s + 1, :]
            od = od.reshape(nrow, A_DK)
            if nrow < c:
                od = jnp.concatenate([od, jnp.zeros((c - nrow, A_DK), F32)], axis=0)
            oa_s[pl.ds(r0, c), sl] = o + od
            kd = (kh * eh[(nlev + 1) * c:(nlev + 2) * c]).astype(BF16)
            st_new = st * eh[c - 1:c] + _dot_tn(vb, kd)
            if per_chunk:
                sfin_ref[ci, hd] = st_new.T
            else:
                st_ref[hd] = st_new
        return carry

    lax.fori_loop(0, tl // c, chunk, 0, unroll=True)

    o_a = _head_norm_gate(oa_s[...], hng_ref[...], p[:, 3 * A_WIDTH:4 * A_WIDTH])
    ug = _gelu(p[:, 4 * A_WIDTH:4 * A_WIDTH + B_WIDTH])
    vn = _gmlp_norm(p[:, 4 * A_WIDTH + B_WIDTH:], lng_ref[...], lnb_ref[...])
    vn_ref[0] = vn
    sg = []
    for gi in range(B_GROUPS):
        sl = slice(gi * B_CH, (gi + 1) * B_CH)
        wsg = ws_ref[gi][0:gchunk, 0:gchunk]
        bsg = bsb_ref[gi][0:gchunk, :]
        rows = []
        for cc in range(tl // gchunk):
            vcc = vn[cc * gchunk:(cc + 1) * gchunk, sl].astype(BF16)
            rows.append(_dot(wsg, vcc) + bsg)
        sg.append(jnp.concatenate(rows, axis=0))
    o_b = ug * jnp.concatenate(sg, axis=-1)
    y = _dot(o_a.astype(BF16), wout_ref[0:A_WIDTH, :]) + _dot(o_b.astype(BF16), wout_ref[A_WIDTH:, :])
    x1_ref[0] = x + y

    if not per_chunk:
        @pl.when(j == pl.num_programs(1) - 1)
        def _():
            for hd in range(A_HEADS):
                sfin_ref[0, hd] = st_ref[hd].T


def _mixer_weights(norm_mix_g, w_in, lb_logits, hgrn_norm_g, ln_g, ln_b, w_s, b_s, w_out):
    mst, msk = _hgrn_consts()
    ws_bf = jnp.tril(w_s).astype(BF16)
    bsb = jnp.broadcast_to(b_s[:, :, None], (B_GROUPS, GMLP_CHUNK, B_CH))
    return (norm_mix_g.reshape(1, D_MODEL), w_in.astype(BF16), lb_logits, jnp.asarray(mst, BF16), jnp.asarray(msk, F32),
            hgrn_norm_g.reshape(1, A_WIDTH), ln_g.reshape(1, B_WIDTH), ln_b.reshape(1, B_WIDTH), ws_bf, bsb,
            w_out.astype(BF16))


def _mixer(x, s0, weights, *, tl, nvalid, per_chunk):
    nb, seq, d = x.shape
    c = HGRN_CHUNK
    spb = tl // c if per_chunk else 1
    gchunk = c if per_chunk else GMLP_CHUNK
    blk = pl.BlockSpec((1, tl, d), lambda b, j: (b, j, 0))
    sblk = pl.BlockSpec((spb, A_HEADS, A_DK, A_DK), lambda b, j: (b, 0, 0, 0))
    act = pltpu.VMEM((tl, A_WIDTH), F32)
    return pl.pallas_call(
        functools.partial(_mixer_kernel, tl=tl, nvalid=nvalid, per_chunk=per_chunk, gchunk=gchunk),
        grid=(nb, seq // tl),
        in_specs=[blk, sblk] + [_const_spec(w.shape) for w in weights],
        out_specs=[blk, sblk, pl.BlockSpec((1, tl, B_WIDTH), lambda b, j: (b, j, 0))],
        out_shape=[jax.ShapeDtypeStruct((nb, seq, d), F32),
                   jax.ShapeDtypeStruct(s0.shape, F32),
                   jax.ShapeDtypeStruct((nb, seq, B_WIDTH), F32)],
        scratch_shapes=[pltpu.VMEM((A_HEADS, A_DK, A_DK), F32), act, act, act, act, act],
        compiler_params=_params(("arbitrary", "arbitrary")),
        name="mixer",
    )(x, s0, *weights)


def _attend(q, k, v):
    s = _dot_nt(q.astype(BF16), k.astype(BF16)) * (X_HD ** -0.5)
    pexp = jnp.exp(s - jnp.max(s, axis=-1, keepdims=True))
    pr = pexp / jnp.sum(pexp, axis=-1, keepdims=True)
    return _dot(pr.astype(BF16), v.astype(BF16))


def _attn_tail(x2, gf_ref, wr_ref, br_ref, x2_ref, hn_ref, lg_ref):
    x2_ref[0] = x2
    hn = _rms(x2, gf_ref[...]).astype(BF16)
    hn_ref[0] = hn
    lg_ref[0] = _dot(hn, wr_ref[...]) + br_ref[...]


def _attn_kernel(x_ref, k_ref, v_ref, gx_ref, wq_ref, wo_ref, gf_ref, wr_ref, br_ref, x2_ref, hn_ref, lg_ref):
    x = x_ref[0]
    h = _rms(x, gx_ref[...]).astype(BF16)
    q = _dot(h, wq_ref[...])
    outs = []
    for hd in range(X_HEADS):
        sl = slice(hd * X_HD, (hd + 1) * X_HD)
        outs.append(_attend(q[:, sl], k_ref[0, :, sl], v_ref[0, :, sl]))
    o = jnp.concatenate(outs, axis=-1).astype(BF16)
    _attn_tail(x + _dot(o, wo_ref[...]), gf_ref, wr_ref, br_ref, x2_ref, hn_ref, lg_ref)


def _attn_by_head_kernel(x_ref, k_ref, v_ref, gx_ref, wq_ref, wo_ref, gf_ref, wr_ref, br_ref, x2_ref, hn_ref, lg_ref,
                         *, nseq):
    x = x_ref[0]
    rows = x.shape[0] // nseq
    h = _rms(x, gx_ref[...]).astype(BF16)
    q = _dot(h, wq_ref[...])
    per_seq = []
    for g in range(nseq):
        qg = q[g * rows:(g + 1) * rows]
        outs = [_attend(qg[:, hd * X_HD:(hd + 1) * X_HD], k_ref[g, :, hd, :], v_ref[g, :, hd, :])
                for hd in range(X_HEADS)]
        per_seq.append(jnp.concatenate(outs, axis=-1))
    o = jnp.concatenate(per_seq, axis=0).astype(BF16)
    _attn_tail(x + _dot(o, wo_ref[...]), gf_ref, wr_ref, br_ref, x2_ref, hn_ref, lg_ref)


def _attn_weights(norm_x_g, w_xq, w_xo, norm_ffn_g, w_router, b_router):
    wr = jnp.zeros((D_MODEL, LANES), BF16).at[:, :N_EXPERTS].set(w_router.astype(BF16))
    br = jnp.full((1, LANES), -1e30, F32).at[0, :N_EXPERTS].set(b_router)
    return (norm_x_g.reshape(1, D_MODEL), w_xq.astype(BF16), w_xo.astype(BF16), norm_ffn_g.reshape(1, D_MODEL), wr, br)


def _attn_out(nb, seq, d):
    return [jax.ShapeDtypeStruct((nb, seq, d), F32), jax.ShapeDtypeStruct((nb, seq, d), BF16),
            jax.ShapeDtypeStruct((nb, seq, LANES), F32)]


def _attn(x, mem_k, mem_v, weights, *, tl):
    nb, seq, d = x.shape
    blk = pl.BlockSpec((1, tl, d), lambda b, j: (b, j, 0))
    kv = pl.BlockSpec((1, MEM_LEN, d), lambda b, j: (b, 0, 0))
    return pl.pallas_call(
        _attn_kernel,
        grid=(nb, seq // tl),
        in_specs=[blk, kv, kv] + [_const_spec(w.shape) for w in weights],
        out_specs=[blk, blk, pl.BlockSpec((1, tl, LANES), lambda b, j: (b, j, 0))],
        out_shape=_attn_out(nb, seq, d),
        compiler_params=_params(("arbitrary", "arbitrary")),
        name="attn",
    )(x, mem_k, mem_v, *weights)


def _attn_by_head(x, mem_k, mem_v, weights, *, nseq):
    nb, rows, d = x.shape
    blk = pl.BlockSpec((1, rows, d), lambda b: (b, 0, 0))
    kv = pl.BlockSpec((None, nseq, MEM_LEN, X_HEADS, X_HD), lambda b: (0, b, 0, 0, 0))
    return pl.pallas_call(
        functools.partial(_attn_by_head_kernel, nseq=nseq),
        grid=(nb,),
        in_specs=[blk, kv, kv] + [_const_spec(w.shape) for w in weights],
        out_specs=[blk, blk, pl.BlockSpec((1, rows, LANES), lambda b: (b, 0, 0))],
        out_shape=_attn_out(nb, rows, d),
        compiler_params=_params(("arbitrary",)),
        name="attn_by_head",
    )(x, mem_k, mem_v, *weights)


def _route_kernel(lg_ref, gm_ref):
    work = lg_ref[...]
    lane = lax.broadcasted_iota(jnp.int32, work.shape, 1).astype(F32)
    vals, hots = [], []
    for _ in range(TOP_K):
        m = jnp.max(work, axis=-1, keepdims=True)
        idx = jnp.min(jnp.where(work == m, lane, float(LANES)), axis=-1, keepdims=True)
        hot = lane == idx
        vals.append(m)
        hots.append(hot)
        work = jnp.where(hot, -jnp.inf, work)
    es = [jnp.exp(v - vals[0]) for v in vals]
    tot = es[0] + es[1] + es[2] + es[3]
    gm = jnp.zeros(work.shape, F32)
    for hot, e in zip(hots, es):
        gm = jnp.where(hot, e / tot, gm)
    gm_ref[...] = gm


def _route(logits, tr=512):
    t = logits.shape[0]
    blk = pl.BlockSpec((tr, LANES), lambda i: (i, 0))
    return pl.pallas_call(
        _route_kernel, grid=(t // tr,), in_specs=[blk], out_specs=blk,
        out_shape=jax.ShapeDtypeStruct((t, LANES), F32),
        compiler_params=_params(("arbitrary",)), name="route",
    )(logits)


MOE_TB = 1536
MOE_CAP = 224


def _moe_kernel(xn_ref, gm_ref, w1_ref, b1_ref, w2_ref, b2_ref, y_ref, g_s, c_s, gt_s, ct_s):
    e = pl.program_id(1)
    tb, cap = MOE_TB, MOE_CAP

    @pl.when(e == 0)
    def _():
        gm = gm_ref[...]
        hot = (gm > 0.0).astype(BF16)
        r = lax.broadcasted_iota(jnp.int32, (tb, tb), 0)
        cidx = lax.broadcasted_iota(jnp.int32, (tb, tb), 1)
        low = (cidx < r).astype(BF16)
        g_s[...] = gm
        c_s[...] = _dot(low, hot)
        gt = gm.T
        gt_s[...] = gt
        up = (r < cidx).astype(BF16)
        ct_s[...] = _dot((gt > 0.0).astype(BF16), up)
        y_ref[...] = jnp.zeros_like(y_ref)

    g_row = gt_s[pl.ds(e, 1), :]
    c_row = ct_s[pl.ds(e, 1), :]
    n = jnp.sum((g_row > 0.0).astype(F32)).astype(jnp.int32)
    sub_c = lax.broadcasted_iota(jnp.int32, (cap, 1), 0).astype(F32)
    lane_c = lax.broadcasted_iota(jnp.int32, (1, cap), 1).astype(F32)

    def one_pass(s, carry):
        pick = lax.broadcasted_iota(jnp.int32, (tb, LANES), 1) == e
        g_col = jnp.sum(jnp.where(pick, g_s[...], 0.0), axis=-1, keepdims=True)
        c_col = jnp.sum(jnp.where(pick, c_s[...], 0.0), axis=-1, keepdims=True)
        base = (s * cap).astype(F32)
        hit = (g_row > 0.0) & (c_row - base == sub_c)
        p = jnp.where(hit, 1.0, 0.0).astype(BF16)
        g_rows = jnp.sum(jnp.where(hit, g_row, 0.0), axis=-1, keepdims=True)
        xin = _dot(p, xn_ref[...]).astype(BF16)
        hh = _dot(xin, w1_ref[0]) + b1_ref[0]
        glu = jnp.minimum(hh[:, :D_FF], SWIGLU_LIMIT)
        lin = jnp.clip(hh[:, D_FF:], -SWIGLU_LIMIT, SWIGLU_LIMIT)
        act = glu * _sigmoid(SWIGLU_ALPHA * glu) * (lin + 1.0)
        out = (_dot(act.astype(BF16), w2_ref[0]) + b2_ref[0]) * g_rows
        hit_t = (g_col > 0.0) & (c_col - base == lane_c)
        pt = jnp.where(hit_t, 1.0, 0.0).astype(BF16)
        y_ref[...] += _dot(pt, out.astype(BF16))
        return carry

    lax.fori_loop(0, (n + cap - 1) // cap, one_pass, 0)


def _moe(xn, gm, w1_bf, b1, w2_bf, b2):
    t, d = xn.shape
    tb = MOE_TB
    return pl.pallas_call(
        _moe_kernel,
        grid=(t // tb, N_EXPERTS),
        in_specs=[pl.BlockSpec((tb, d), lambda i, e: (i, 0)),
                  pl.BlockSpec((tb, LANES), lambda i, e: (i, 0)),
                  pl.BlockSpec((1, d, 2 * D_FF), lambda i, e: (e, 0, 0)),
                  pl.BlockSpec((1, 1, 2 * D_FF), lambda i, e: (e, 0, 0)),
                  pl.BlockSpec((1, D_FF, d), lambda i, e: (e, 0, 0)),
                  pl.BlockSpec((1, 1, d), lambda i, e: (e, 0, 0))],
        out_specs=pl.BlockSpec((tb, d), lambda i, e: (i, 0)),
        out_shape=jax.ShapeDtypeStruct((t, d), F32),
        scratch_shapes=[pltpu.VMEM((tb, LANES), F32), pltpu.VMEM((tb, LANES), F32),
                        pltpu.VMEM((LANES, tb), F32), pltpu.VMEM((LANES, tb), F32)],
        compiler_params=_params(("arbitrary", "arbitrary")),
        name="moe",
    )(xn, gm, w1_bf, b1, w2_bf, b2)


def _final_kernel(x_ref, y_ref, g_ref, o_ref):
    o_ref[...] = _rms(x_ref[...] + y_ref[...], g_ref[...])


def _final(x2, y_all, row0, g, tf=512):
    t, d = x2.shape
    off = row0 // tf
    return pl.pallas_call(
        _final_kernel, grid=(t // tf,),
        in_specs=[pl.BlockSpec((tf, d), lambda i: (i, 0)), pl.BlockSpec((tf, d), lambda i: (i + off, 0)),
                  _const_spec((1, d))],
        out_specs=pl.BlockSpec((tf, d), lambda i: (i, 0)),
        out_shape=jax.ShapeDtypeStruct((t, d), F32),
        compiler_params=_params(("arbitrary",)), name="final",
    )(x2, y_all, g)


SAMPLE_PAD = HGRN_CHUNK
STEP_ROWS = 256
ATTN_SEQS = 2


def kernel(x_prompt, x_sample, mem_prompt, state_hgrn, cache_mem_k, cache_mem_v, norm_mix_g, w_in, hgrn_lb_logits,
           hgrn_norm_g, gmlp_ln_g, gmlp_ln_b, gmlp_w_s, gmlp_b_s, w_out, norm_x_g, norm_mem_g, w_xq, w_xk, w_xv, w_xo,
           norm_ffn_g, w_router, b_router, w1, b1, w2, b2, norm_final_g):
    assert w_in.shape[0] == 1, "one layer"
    nbp, seq, d = x_prompt.shape
    nbs, dseq, _ = x_sample.shape
    mw = _mixer_weights(norm_mix_g[0], w_in[0], hgrn_lb_logits, hgrn_norm_g[0], gmlp_ln_g[0], gmlp_ln_b[0],
                        gmlp_w_s[0], gmlp_b_s[0], w_out[0])
    aw = _attn_weights(norm_x_g[0], w_xq[0], w_xo[0], norm_ffn_g[0], w_router[0], b_router[0])

    mk, mv, mkb, mvb = _memkv(mem_prompt, norm_mem_g[0].reshape(1, d), w_xk[0].astype(BF16), w_xv[0].astype(BF16))
    s0p = jnp.zeros((nbp, A_HEADS, A_DK, A_DK), F32)
    x1p, s_p, _ = _mixer(x_prompt, s0p, mw, tl=STEP_ROWS, nvalid=HGRN_CHUNK, per_chunk=False)
    x2p, hnp, lgp = _attn(x1p, mkb, mvb, aw, tl=STEP_ROWS)

    spb = STEP_ROWS // SAMPLE_PAD
    xs = jnp.pad(x_sample, ((0, 0), (0, SAMPLE_PAD - dseq), (0, 0))).reshape(nbs // spb, STEP_ROWS, d)
    x1s, s_s, vns = _mixer(xs, state_hgrn[0], mw, tl=STEP_ROWS, nvalid=dseq, per_chunk=True)
    x2s, hns, lgs = _attn_by_head(x1s.reshape(nbs // ATTN_SEQS, ATTN_SEQS * SAMPLE_PAD, d), cache_mem_k, cache_mem_v,
                                  aw, nseq=ATTN_SEQS)

    def real(a):
        return a.reshape(nbs, SAMPLE_PAD, a.shape[-1])[:, :dseq].reshape(nbs * dseq, a.shape[-1])

    tp = nbp * seq

    xn = jnp.concatenate([hnp.reshape(tp, d), real(hns)], axis=0)
    gm = _route(jnp.concatenate([lgp.reshape(tp, LANES), real(lgs)], axis=0))
    y_all = _moe(xn, gm, w1[0].astype(BF16), b1[0].reshape(N_EXPERTS, 1, 2 * D_FF), w2[0].astype(BF16),
                 b2[0].reshape(N_EXPERTS, 1, d))
    gfin = norm_final_g.reshape(1, d)
    y_prompt = _final(x2p.reshape(tp, d), y_all, 0, gfin).reshape(nbp, seq, d)
    y_sample = _final(real(x2s), y_all, tp, gfin).reshape(nbs, dseq, d)

    vn_s = real(vns).reshape(1, nbs, dseq, B_GROUPS, B_CH)
    return (y_prompt, y_sample, s_p[None], mk.reshape(1, nbp, MEM_LEN, X_HEADS, X_HD),
            mv.reshape(1, nbp, MEM_LEN, X_HEADS, X_HD), s_s[None], vn_s)
```

```python
import functools
import math

import jax
import jax.numpy as jnp
import numpy as np
from jax import lax
from jax.experimental import pallas as pl
from jax.experimental.pallas import tpu as pltpu

F32 = jnp.float32
BF16 = jnp.bfloat16

D_MODEL = 1024
A_HEADS = 4
A_DK = 128
A_WIDTH = 512
B_GROUPS = 4
B_CH = 128
B_WIDTH = 512
IN_COLS = 4 * A_WIDTH + 2 * B_WIDTH
HGRN_CHUNK = 64
GMLP_CHUNK = 128
MEM_LEN = 256
X_HEADS = 4
X_HD = 256
N_EXPERTS = 32
TOP_K = 4
D_FF = 1024
SWIGLU_LIMIT = 7.0
SWIGLU_ALPHA = 1.702
EPS = 1e-6

LANES = 128
VMEM_LIMIT = 56 * 1024 * 1024

NT = (((1,), (1,)), ((), ()))
TN = (((0,), (0,)), ((), ()))


def _dot(a, b):
    return jnp.dot(a, b, preferred_element_type=F32)


def _dot_nt(a, b):
    return lax.dot_general(a, b, NT, preferred_element_type=F32)


def _dot_tn(a, b):
    return lax.dot_general(a, b, TN, preferred_element_type=F32)


def _sigmoid(x):
    return 1.0 / (1.0 + jnp.exp(-x))


def _silu(x):
    return x * _sigmoid(x)


def _gelu(x):
    c = math.sqrt(2.0 / math.pi)
    return x * (0.5 * (1.0 + jnp.tanh(c * (x + 0.044715 * (x * x * x)))))


def _rms(x, g):
    return x * lax.rsqrt(jnp.mean(x * x, axis=-1, keepdims=True) + EPS) * g


def _split_bf16(x):
    hi = x.astype(BF16)
    lo = (x - hi.astype(F32)).astype(BF16)
    return hi, lo


def _params(sem):
    return pltpu.CompilerParams(dimension_semantics=sem, vmem_limit_bytes=VMEM_LIMIT)


def _const_spec(shape):
    nd = len(shape)
    return pl.BlockSpec(shape, lambda *_: (0,) * nd)


def _memkv_kernel(mem_ref, g_ref, wk_ref, wv_ref, k_ref, v_ref, kb_ref, vb_ref):
    mn = _rms(mem_ref[0], g_ref[...]).astype(BF16)
    k = _dot(mn, wk_ref[...])
    v = _dot(mn, wv_ref[...])
    k_ref[0] = k
    v_ref[0] = v
    kb_ref[0] = k.astype(BF16)
    vb_ref[0] = v.astype(BF16)


def _memkv(mem, g, wk_bf, wv_bf):
    nb, m, d = mem.shape
    blk = pl.BlockSpec((1, m, d), lambda b: (b, 0, 0))
    return pl.pallas_call(
        _memkv_kernel,
        grid=(nb,),
        in_specs=[blk, _const_spec((1, d)), _const_spec((d, d)), _const_spec((d, d))],
        out_specs=[blk, blk, blk, blk],
        out_shape=[jax.ShapeDtypeStruct((nb, m, d), F32)] * 2 + [jax.ShapeDtypeStruct((nb, m, d), BF16)] * 2,
        compiler_params=_params(("arbitrary",)),
        name="memkv",
    )(mem, g, wk_bf, wv_bf)


_LEVELS = (32, 16, 8)
_DIAG = 8


def _hgrn_consts():
    c = HGRN_CHUNK
    t = np.arange(c)
    rows = [(t[None, :] <= t[:, None]).astype(np.float32)]
    masks = []
    for h in _LEVELS:
        m = np.zeros((c, c), np.float32)
        for r in range(c):
            ref = (r // (2 * h)) * 2 * h + h - 1
            if r % (2 * h) >= h:
                m[r, ref + 1:r + 1] = 1.0
            else:
                m[r, r + 1:ref + 1] = 1.0
        rows.append(m)
        tt, ss = t[:, None], t[None, :]
        masks.append(((tt // (2 * h) == ss // (2 * h)) & (tt % (2 * h) >= h) & (ss % (2 * h) < h)).astype(np.float32))
    rows.append((t[None, :] > t[:, None]).astype(np.float32))
    return np.concatenate(rows, 0), np.stack(masks)


def _lower_bound(lbl):
    mx = jnp.max(lbl, axis=0, keepdims=True)
    e = jnp.exp(lbl - mx)
    return e[0:1] / jnp.sum(e, axis=0, keepdims=True)


def _hgrn_gates(p, lb):
    q = _silu(p[:, 0:A_WIDTH])
    fg = lb + (1.0 - lb) * _sigmoid(p[:, A_WIDTH:2 * A_WIDTH])
    return q, 1.0 - fg, jnp.log(fg), p[:, 2 * A_WIDTH:3 * A_WIDTH]


def _head_norm_gate(o, hng, g):
    outs = []
    for hd in range(A_HEADS):
        sl = slice(hd * A_DK, (hd + 1) * A_DK)
        oh = o[:, sl]
        outs.append(oh * lax.rsqrt(jnp.mean(oh * oh, axis=-1, keepdims=True) + EPS) * hng[:, sl])
    return jnp.concatenate(outs, axis=-1) * _silu(g)


def _gmlp_norm(v, lng, lnb):
    vf = _gelu(v)
    outs = []
    for gi in range(B_GROUPS):
        sl = slice(gi * B_CH, (gi + 1) * B_CH)
        vg = vf[:, sl]
        mu = jnp.mean(vg, axis=-1, keepdims=True)
        dv = vg - mu
        var = jnp.mean(dv * dv, axis=-1, keepdims=True)
        outs.append(dv * lax.rsqrt(var + EPS) * lng[:, sl] + lnb[:, sl])
    return jnp.concatenate(outs, axis=-1)


def _mixer_kernel(x_ref, s0_ref, gmix_ref, win_ref, lbl_ref, mst_ref, msk_ref, hng_ref, lng_ref, lnb_ref,
                  ws_ref, bsb_ref, wout_ref, x1_ref, sfin_ref, vn_ref,
                  st_ref, q_s, k_s, v_s, lf_s, oa_s, *, tl, nvalid, per_chunk, gchunk):
    j = pl.program_id(1)
    c = HGRN_CHUNK

    if not per_chunk:
        @pl.when(j == 0)
        def _():
            for hd in range(A_HEADS):
                st_ref[hd] = s0_ref[0, hd].T

    x = x_ref[0]
    h = _rms(x, gmix_ref[...]).astype(BF16)
    p = _dot(h, win_ref[...])
    lb = _lower_bound(lbl_ref[...])
    q, k, lf, v = _hgrn_gates(p, lb)
    if nvalid < c:
        row = lax.broadcasted_iota(jnp.int32, (tl, 1), 0)
        lf = jnp.where(row % c < nvalid, lf, 0.0)
    q_s[...] = q
    k_s[...] = k
    lf_s[...] = lf
    v_s[...] = v

    sub = lax.broadcasted_iota(jnp.int32, (c // _DIAG, _DIAG, 1), 1)

    def chunk(ci, carry):
        r0 = pl.multiple_of(ci * c, c)
        lfc = lf_s[pl.ds(r0, c), :]
        hi, lo = _split_bf16(lfc)
        mst = mst_ref[...]
        short = nvalid <= _DIAG
        nlev = 0 if short else len(_LEVELS)
        if short:
            mst = jnp.concatenate([mst[0:c], mst[4 * c:5 * c]], axis=0)
        bexp = _dot(mst, hi) + _dot(mst, lo)
        e = jnp.exp(bexp)
        qc = q_s[pl.ds(r0, c), :]
        kc = k_s[pl.ds(r0, c), :]
        vc = v_s[pl.ds(r0, c), :]
        for hd in range(A_HEADS):
            sl = slice(hd * A_DK, (hd + 1) * A_DK)
            qh, kh, vh, eh = qc[:, sl], kc[:, sl], vc[:, sl], e[:, sl]
            vb = vh.astype(BF16)
            st = s0_ref[ci, hd].T if per_chunk else st_ref[hd]
            o = jnp.zeros((c, A_DK), F32)
            if nlev:
                sc = jnp.zeros((c, c), F32)
                for li in range(nlev):
                    el = eh[(li + 1) * c:(li + 2) * c]
                    sc = sc + _dot_nt((qh * el).astype(BF16), (kh * el).astype(BF16)) * msk_ref[li]
                o = _dot(sc.astype(BF16), vb)
            o = o + _dot_nt((qh * eh[0:c]).astype(BF16), st.astype(BF16))
            nblk = 1 if short else c // _DIAG
            nrow = nblk * _DIAG
            b3 = bexp[0:nrow, sl].reshape(nblk, _DIAG, A_DK)
            q3 = qh[0:nrow].reshape(nblk, _DIAG, A_DK)
            k3 = kh[0:nrow].reshape(nblk, _DIAG, A_DK)
            v3 = vh[0:nrow].reshape(nblk, _DIAG, A_DK)
            od = jnp.zeros((nblk, _DIAG, A_DK), F32)
            for s in range(min(nvalid, _DIAG)):
                dec = jnp.exp(jnp.minimum(b3 - b3[:, s:s + 1, :], 0.0))
                w = jnp.sum(q3 * dec * k3[:, s:s + 1, :], axis=-1, keepdims=True)
                w = jnp.where(sub[0:nblk] >= s, w, 0.0)
                od = od + w * v3[:, s:s + 1, :]
            od = od.reshape(nrow, A_DK)
            if nrow < c:
                od = jnp.concatenate([od, jnp.zeros((c - nrow, A_DK), F32)], axis=0)
            oa_s[pl.ds(r0, c), sl] = o + od
            kd = (kh * eh[(nlev + 1) * c:(nlev + 2) * c]).astype(BF16)
            st_new = st * eh[c - 1:c] + _dot_tn(vb, kd)
            if per_chunk:
                sfin_ref[ci, hd] = st_new.T
            else:
                st_ref[hd] = st_new
        return carry

    lax.fori_loop(0, tl // c, chunk, 0, unroll=True)

    o_a = _head_norm_gate(oa_s[...], hng_ref[...], p[:, 3 * A_WIDTH:4 * A_WIDTH])
    ug = _gelu(p[:, 4 * A_WIDTH:4 * A_WIDTH + B_WIDTH])
    vn = _gmlp_norm(p[:, 4 * A_WIDTH + B_WIDTH:], lng_ref[...], lnb_ref[...])
    vn_ref[0] = vn
    sg = []
    for gi in range(B_GROUPS):
        sl = slice(gi * B_CH, (gi + 1) * B_CH)
        wsg = ws_ref[gi][0:gchunk, 0:gchunk]
        bsg = bsb_ref[gi][0:gchunk, :]
        rows = []
        for cc in range(tl // gchunk):
            vcc = vn[cc * gchunk:(cc + 1) * gchunk, sl].astype(BF16)
            rows.append(_dot(wsg, vcc) + bsg)
        sg.append(jnp.concatenate(rows, axis=0))
    o_b = ug * jnp.concatenate(sg, axis=-1)
    y = _dot(o_a.astype(BF16), wout_ref[0:A_WIDTH, :]) + _dot(o_b.astype(BF16), wout_ref[A_WIDTH:, :])
    x1_ref[0] = x + y

    if not per_chunk:
        @pl.when(j == pl.num_programs(1) - 1)
        def _():
            for hd in range(A_HEADS):
                sfin_ref[0, hd] = st_ref[hd].T


def _mixer_weights(norm_mix_g, w_in, lb_logits, hgrn_norm_g, ln_g, ln_b, w_s, b_s, w_out):
    mst, msk = _hgrn_consts()
    ws_bf = jnp.tril(w_s).astype(BF16)
    bsb = jnp.broadcast_to(b_s[:, :, None], (B_GROUPS, GMLP_CHUNK, B_CH))
    return (norm_mix_g.reshape(1, D_MODEL), w_in.astype(BF16), lb_logits, jnp.asarray(mst, BF16), jnp.asarray(msk, F32),
            hgrn_norm_g.reshape(1, A_WIDTH), ln_g.reshape(1, B_WIDTH), ln_b.reshape(1, B_WIDTH), ws_bf, bsb,
            w_out.astype(BF16))


def _mixer(x, s0, weights, *, tl, nvalid, per_chunk):
    nb, seq, d = x.shape
    c = HGRN_CHUNK
    spb = tl // c if per_chunk else 1
    gchunk = c if per_chunk else GMLP_CHUNK
    blk = pl.BlockSpec((1, tl, d), lambda b, j: (b, j, 0))
    sblk = pl.BlockSpec((spb, A_HEADS, A_DK, A_DK), lambda b, j: (b, 0, 0, 0))
    act = pltpu.VMEM((tl, A_WIDTH), F32)
    return pl.pallas_call(
        functools.partial(_mixer_kernel, tl=tl, nvalid=nvalid, per_chunk=per_chunk, gchunk=gchunk),
        grid=(nb, seq // tl),
        in_specs=[blk, sblk] + [_const_spec(w.shape) for w in weights],
        out_specs=[blk, sblk, pl.BlockSpec((1, tl, B_WIDTH), lambda b, j: (b, j, 0))],
        out_shape=[jax.ShapeDtypeStruct((nb, seq, d), F32),
                   jax.ShapeDtypeStruct(s0.shape, F32),
                   jax.ShapeDtypeStruct((nb, seq, B_WIDTH), F32)],
        scratch_shapes=[pltpu.VMEM((A_HEADS, A_DK, A_DK), F32), act, act, act, act, act],
        compiler_params=_params(("arbitrary", "arbitrary")),
        name="mixer",
    )(x, s0, *weights)


def _attend(q, k, v):
    s = _dot_nt(q.astype(BF16), k.astype(BF16)) * (X_HD ** -0.5)
    pexp = jnp.exp(s - jnp.max(s, axis=-1, keepdims=True))
    pr = pexp / jnp.sum(pexp, axis=-1, keepdims=True)
    return _dot(pr.astype(BF16), v.astype(BF16))


def _attn_tail(x2, gf_ref, wr_ref, br_ref, x2_ref, hn_ref, lg_ref):
    x2_ref[0] = x2
    hn = _rms(x2, gf_ref[...]).astype(BF16)
    hn_ref[0] = hn
    lg_ref[0] = _dot(hn, wr_ref[...]) + br_ref[...]


def _attn_kernel(x_ref, k_ref, v_ref, gx_ref, wq_ref, wo_ref, gf_ref, wr_ref, br_ref, x2_ref, hn_ref, lg_ref,
                 *, nseq):
    x = x_ref[0]
    rows = x.shape[0] // nseq
    h = _rms(x, gx_ref[...]).astype(BF16)
    q = _dot(h, wq_ref[...])
    per_seq = []
    for g in range(nseq):
        qg = q[g * rows:(g + 1) * rows]
        outs = []
        for hd in range(X_HEADS):
            sl = slice(hd * X_HD, (hd + 1) * X_HD)
            outs.append(_attend(qg[:, sl], k_ref[g, :, sl], v_ref[g, :, sl]))
        per_seq.append(jnp.concatenate(outs, axis=-1))
    o = jnp.concatenate(per_seq, axis=0).astype(BF16)
    _attn_tail(x + _dot(o, wo_ref[...]), gf_ref, wr_ref, br_ref, x2_ref, hn_ref, lg_ref)


def _attn_weights(norm_x_g, w_xq, w_xo, norm_ffn_g, w_router, b_router):
    wr = jnp.zeros((D_MODEL, LANES), BF16).at[:, :N_EXPERTS].set(w_router.astype(BF16))
    br = jnp.full((1, LANES), -1e30, F32).at[0, :N_EXPERTS].set(b_router)
    return (norm_x_g.reshape(1, D_MODEL), w_xq.astype(BF16), w_xo.astype(BF16), norm_ffn_g.reshape(1, D_MODEL), wr, br)


def _attn(x, mem_k, mem_v, weights, *, tl, nseq):
    nb, seq, d = x.shape
    assert nseq == 1 or seq == tl
    blk = pl.BlockSpec((1, tl, d), lambda b, j: (b, j, 0))
    kv = pl.BlockSpec((nseq, MEM_LEN, d), lambda b, j: (b, 0, 0))
    return pl.pallas_call(
        functools.partial(_attn_kernel, nseq=nseq),
        grid=(nb, seq // tl),
        in_specs=[blk, kv, kv] + [_const_spec(w.shape) for w in weights],
        out_specs=[blk, blk, pl.BlockSpec((1, tl, LANES), lambda b, j: (b, j, 0))],
        out_shape=[jax.ShapeDtypeStruct((nb, seq, d), F32), jax.ShapeDtypeStruct((nb, seq, d), BF16),
                   jax.ShapeDtypeStruct((nb, seq, LANES), F32)],
        compiler_params=_params(("arbitrary", "arbitrary")),
        name="attn",
    )(x, mem_k, mem_v, *weights)


def _route_kernel(lg_ref, gm_ref):
    work = lg_ref[...]
    lane = lax.broadcasted_iota(jnp.int32, work.shape, 1).astype(F32)
    vals, hots = [], []
    for _ in range(TOP_K):
        m = jnp.max(work, axis=-1, keepdims=True)
        idx = jnp.min(jnp.where(work == m, lane, float(LANES)), axis=-1, keepdims=True)
        hot = lane == idx
        vals.append(m)
        hots.append(hot)
        work = jnp.where(hot, -jnp.inf, work)
    es = [jnp.exp(v - vals[0]) for v in vals]
    tot = es[0] + es[1] + es[2] + es[3]
    gm = jnp.zeros(work.shape, F32)
    for hot, e in zip(hots, es):
        gm = jnp.where(hot, e / tot, gm)
    gm_ref[...] = gm


def _route(logits, tr=512):
    t = logits.shape[0]
    blk = pl.BlockSpec((tr, LANES), lambda i: (i, 0))
    return pl.pallas_call(
        _route_kernel, grid=(t // tr,), in_specs=[blk], out_specs=blk,
        out_shape=jax.ShapeDtypeStruct((t, LANES), F32),
        compiler_params=_params(("arbitrary",)), name="route",
    )(logits)


MOE_TB = 1536
MOE_CAP = 224


def _moe_kernel(xn_ref, gm_ref, w1_ref, b1_ref, w2_ref, b2_ref, y_ref, g_s, c_s, gt_s, ct_s):
    e = pl.program_id(1)
    tb, cap = MOE_TB, MOE_CAP

    @pl.when(e == 0)
    def _():
        gm = gm_ref[...]
        hot = (gm > 0.0).astype(BF16)
        r = lax.broadcasted_iota(jnp.int32, (tb, tb), 0)
        cidx = lax.broadcasted_iota(jnp.int32, (tb, tb), 1)
        low = (cidx < r).astype(BF16)
        g_s[...] = gm
        c_s[...] = _dot(low, hot)
        gt = gm.T
        gt_s[...] = gt
        up = (r < cidx).astype(BF16)
        ct_s[...] = _dot((gt > 0.0).astype(BF16), up)
        y_ref[...] = jnp.zeros_like(y_ref)

    g_row = gt_s[pl.ds(e, 1), :]
    c_row = ct_s[pl.ds(e, 1), :]
    n = jnp.sum((g_row > 0.0).astype(F32)).astype(jnp.int32)
    sub_c = lax.broadcasted_iota(jnp.int32, (cap, 1), 0).astype(F32)
    lane_c = lax.broadcasted_iota(jnp.int32, (1, cap), 1).astype(F32)

    def one_pass(s, carry):
        pick = lax.broadcasted_iota(jnp.int32, (tb, LANES), 1) == e
        g_col = jnp.sum(jnp.where(pick, g_s[...], 0.0), axis=-1, keepdims=True)
        c_col = jnp.sum(jnp.where(pick, c_s[...], 0.0), axis=-1, keepdims=True)
        base = (s * cap).astype(F32)
        hit = (g_row > 0.0) & (c_row - base == sub_c)
        p = jnp.where(hit, 1.0, 0.0).astype(BF16)
        g_rows = jnp.sum(jnp.where(hit, g_row, 0.0), axis=-1, keepdims=True)
        xin = _dot(p, xn_ref[...]).astype(BF16)
        hh = _dot(xin, w1_ref[0]) + b1_ref[0]
        glu = jnp.minimum(hh[:, :D_FF], SWIGLU_LIMIT)
        lin = jnp.clip(hh[:, D_FF:], -SWIGLU_LIMIT, SWIGLU_LIMIT)
        act = glu * _sigmoid(SWIGLU_ALPHA * glu) * (lin + 1.0)
        out = (_dot(act.astype(BF16), w2_ref[0]) + b2_ref[0]) * g_rows
        hit_t = (g_col > 0.0) & (c_col - base == lane_c)
        pt = jnp.where(hit_t, 1.0, 0.0).astype(BF16)
        y_ref[...] += _dot(pt, out.astype(BF16))
        return carry

    lax.fori_loop(0, (n + cap - 1) // cap, one_pass, 0)


def _moe(xn, gm, w1_bf, b1, w2_bf, b2):
    t, d = xn.shape
    tb = MOE_TB
    return pl.pallas_call(
        _moe_kernel,
        grid=(t // tb, N_EXPERTS),
        in_specs=[pl.BlockSpec((tb, d), lambda i, e: (i, 0)),
                  pl.BlockSpec((tb, LANES), lambda i, e: (i, 0)),
                  pl.BlockSpec((1, d, 2 * D_FF), lambda i, e: (e, 0, 0)),
                  pl.BlockSpec((1, 1, 2 * D_FF), lambda i, e: (e, 0, 0)),
                  pl.BlockSpec((1, D_FF, d), lambda i, e: (e, 0, 0)),
                  pl.BlockSpec((1, 1, d), lambda i, e: (e, 0, 0))],
        out_specs=pl.BlockSpec((tb, d), lambda i, e: (i, 0)),
        out_shape=jax.ShapeDtypeStruct((t, d), F32),
        scratch_shapes=[pltpu.VMEM((tb, LANES), F32), pltpu.VMEM((tb, LANES), F32),
                        pltpu.VMEM((LANES, tb), F32), pltpu.VMEM((LANES, tb), F32)],
        compiler_params=_params(("arbitrary", "arbitrary")),
        name="moe",
    )(xn, gm, w1_bf, b1, w2_bf, b2)


def _final_kernel(x_ref, y_ref, g_ref, o_ref):
    o_ref[...] = _rms(x_ref[...] + y_ref[...], g_ref[...])


def _final(x2, y_all, row0, g, tf=512):
    t, d = x2.shape
    off = row0 // tf
    return pl.pallas_call(
        _final_kernel, grid=(t // tf,),
        in_specs=[pl.BlockSpec((tf, d), lambda i: (i, 0)), pl.BlockSpec((tf, d), lambda i: (i + off, 0)),
                  _const_spec((1, d))],
        out_specs=pl.BlockSpec((tf, d), lambda i: (i, 0)),
        out_shape=jax.ShapeDtypeStruct((t, d), F32),
        compiler_params=_params(("arbitrary",)), name="final",
    )(x2, y_all, g)


SAMPLE_PAD = HGRN_CHUNK
STEP_ROWS = 256


def kernel(x_prompt, x_sample, mem_prompt, state_hgrn, cache_mem_k, cache_mem_v, norm_mix_g, w_in, hgrn_lb_logits,
           hgrn_norm_g, gmlp_ln_g, gmlp_ln_b, gmlp_w_s, gmlp_b_s, w_out, norm_x_g, norm_mem_g, w_xq, w_xk, w_xv, w_xo,
           norm_ffn_g, w_router, b_router, w1, b1, w2, b2, norm_final_g):
    assert w_in.shape[0] == 1, "one layer"
    nbp, seq, d = x_prompt.shape
    nbs, dseq, _ = x_sample.shape
    mw = _mixer_weights(norm_mix_g[0], w_in[0], hgrn_lb_logits, hgrn_norm_g[0], gmlp_ln_g[0], gmlp_ln_b[0],
                        gmlp_w_s[0], gmlp_b_s[0], w_out[0])
    aw = _attn_weights(norm_x_g[0], w_xq[0], w_xo[0], norm_ffn_g[0], w_router[0], b_router[0])

    mk, mv, mkb, mvb = _memkv(mem_prompt, norm_mem_g[0].reshape(1, d), w_xk[0].astype(BF16), w_xv[0].astype(BF16))
    s0p = jnp.zeros((nbp, A_HEADS, A_DK, A_DK), F32)
    x1p, s_p, _ = _mixer(x_prompt, s0p, mw, tl=STEP_ROWS, nvalid=HGRN_CHUNK, per_chunk=False)
    x2p, hnp, lgp = _attn(x1p, mkb, mvb, aw, tl=STEP_ROWS, nseq=1)

    spb = STEP_ROWS // SAMPLE_PAD
    xs = jnp.pad(x_sample, ((0, 0), (0, SAMPLE_PAD - dseq), (0, 0))).reshape(nbs // spb, STEP_ROWS, d)
    x1s, s_s, vns = _mixer(xs, state_hgrn[0], mw, tl=STEP_ROWS, nvalid=dseq, per_chunk=True)
    ck = cache_mem_k[0].astype(BF16).reshape(nbs, MEM_LEN, d)
    cv = cache_mem_v[0].astype(BF16).reshape(nbs, MEM_LEN, d)
    x2s, hns, lgs = _attn(x1s, ck, cv, aw, tl=STEP_ROWS, nseq=spb)

    def real(a):
        return a.reshape(nbs, SAMPLE_PAD, a.shape[-1])[:, :dseq].reshape(nbs * dseq, a.shape[-1])

    tp = nbp * seq

    xn = jnp.concatenate([hnp.reshape(tp, d), real(hns)], axis=0)
    gm = _route(jnp.concatenate([lgp.reshape(tp, LANES), real(lgs)], axis=0))
    y_all = _moe(xn, gm, w1[0].astype(BF16), b1[0].reshape(N_EXPERTS, 1, 2 * D_FF), w2[0].astype(BF16),
                 b2[0].reshape(N_EXPERTS, 1, d))
    gfin = norm_final_g.reshape(1, d)
    y_prompt = _final(x2p.reshape(tp, d), y_all, 0, gfin).reshape(nbp, seq, d)
    y_sample = _final(real(x2s), y_all, tp, gfin).reshape(nbs, dseq, d)

    vn_s = real(vns).reshape(1, nbs, dseq, B_GROUPS, B_CH)
    return (y_prompt, y_sample, s_p[None], mk.reshape(1, nbp, MEM_LEN, X_HEADS, X_HD),
            mv.reshape(1, nbp, MEM_LEN, X_HEADS, X_HD), s_s[None], vn_s)
```

```python
import functools
import math

import jax
import jax.numpy as jnp
import numpy as np
from jax import lax
from jax.experimental import pallas as pl
from jax.experimental.pallas import tpu as pltpu

F32 = jnp.float32
BF16 = jnp.bfloat16

D_MODEL = 1024
A_HEADS = 4
A_DK = 128
A_WIDTH = 512
B_GROUPS = 4
B_CH = 128
B_WIDTH = 512
IN_COLS = 4 * A_WIDTH + 2 * B_WIDTH
HGRN_CHUNK = 64
GMLP_CHUNK = 128
MEM_LEN = 256
X_HEADS = 4
X_HD = 256
N_EXPERTS = 32
TOP_K = 4
D_FF = 1024
SWIGLU_LIMIT = 7.0
SWIGLU_ALPHA = 1.702
EPS = 1e-6

LANES = 128
VMEM_LIMIT = 56 * 1024 * 1024

NT = (((1,), (1,)), ((), ()))
TN = (((0,), (0,)), ((), ()))


def _dot(a, b):
    return jnp.dot(a, b, preferred_element_type=F32)


def _dot_nt(a, b):
    return lax.dot_general(a, b, NT, preferred_element_type=F32)


def _dot_tn(a, b):
    return lax.dot_general(a, b, TN, preferred_element_type=F32)


def _sigmoid(x):
    return 1.0 / (1.0 + jnp.exp(-x))


def _silu(x):
    return x * _sigmoid(x)


def _gelu(x):
    c = math.sqrt(2.0 / math.pi)
    return x * (0.5 * (1.0 + jnp.tanh(c * (x + 0.044715 * (x * x * x)))))


def _rms(x, g):
    return x * lax.rsqrt(jnp.mean(x * x, axis=-1, keepdims=True) + EPS) * g


def _split_bf16(x):
    hi = x.astype(BF16)
    lo = (x - hi.astype(F32)).astype(BF16)
    return hi, lo


def _params(sem):
    return pltpu.CompilerParams(dimension_semantics=sem, vmem_limit_bytes=VMEM_LIMIT)


def _const_spec(shape):
    nd = len(shape)
    return pl.BlockSpec(shape, lambda *_: (0,) * nd)


def _memkv_kernel(mem_ref, g_ref, wk_ref, wv_ref, k_ref, v_ref, kb_ref, vb_ref):
    mn = _rms(mem_ref[0], g_ref[...]).astype(BF16)
    k = _dot(mn, wk_ref[...])
    v = _dot(mn, wv_ref[...])
    k_ref[0] = k
    v_ref[0] = v
    kb_ref[0] = k.astype(BF16)
    vb_ref[0] = v.astype(BF16)


def _memkv(mem, g, wk_bf, wv_bf):
    nb, m, d = mem.shape
    blk = pl.BlockSpec((1, m, d), lambda b: (b, 0, 0))
    return pl.pallas_call(
        _memkv_kernel,
        grid=(nb,),
        in_specs=[blk, _const_spec((1, d)), _const_spec((d, d)), _const_spec((d, d))],
        out_specs=[blk, blk, blk, blk],
        out_shape=[jax.ShapeDtypeStruct((nb, m, d), F32)] * 2 + [jax.ShapeDtypeStruct((nb, m, d), BF16)] * 2,
        compiler_params=_params(("arbitrary",)),
        name="memkv",
    )(mem, g, wk_bf, wv_bf)


_LEVELS = (32, 16, 8)
_DIAG = 8


def _hgrn_consts():
    c = HGRN_CHUNK
    t = np.arange(c)
    rows = [(t[None, :] <= t[:, None]).astype(np.float32)]
    masks = []
    for h in _LEVELS:
        m = np.zeros((c, c), np.float32)
        for r in range(c):
            ref = (r // (2 * h)) * 2 * h + h - 1
            if r % (2 * h) >= h:
                m[r, ref + 1:r + 1] = 1.0
            else:
                m[r, r + 1:ref + 1] = 1.0
        rows.append(m)
        tt, ss = t[:, None], t[None, :]
        masks.append(((tt // (2 * h) == ss // (2 * h)) & (tt % (2 * h) >= h) & (ss % (2 * h) < h)).astype(np.float32))
    rows.append((t[None, :] > t[:, None]).astype(np.float32))
    return np.concatenate(rows, 0), np.stack(masks)


def _lower_bound(lbl):
    mx = jnp.max(lbl, axis=0, keepdims=True)
    e = jnp.exp(lbl - mx)
    return e[0:1] / jnp.sum(e, axis=0, keepdims=True)


def _hgrn_gates(p, lb):
    q = _silu(p[:, 0:A_WIDTH])
    fg = lb + (1.0 - lb) * _sigmoid(p[:, A_WIDTH:2 * A_WIDTH])
    return q, 1.0 - fg, jnp.log(fg), p[:, 2 * A_WIDTH:3 * A_WIDTH]


def _head_norm_gate(o, hng, g):
    outs = []
    for hd in range(A_HEADS):
        sl = slice(hd * A_DK, (hd + 1) * A_DK)
        oh = o[:, sl]
        outs.append(oh * lax.rsqrt(jnp.mean(oh * oh, axis=-1, keepdims=True) + EPS) * hng[:, sl])
    return jnp.concatenate(outs, axis=-1) * _silu(g)


def _gmlp_norm(v, lng, lnb):
    vf = _gelu(v)
    outs = []
    for gi in range(B_GROUPS):
        sl = slice(gi * B_CH, (gi + 1) * B_CH)
        vg = vf[:, sl]
        mu = jnp.mean(vg, axis=-1, keepdims=True)
        dv = vg - mu
        var = jnp.mean(dv * dv, axis=-1, keepdims=True)
        outs.append(dv * lax.rsqrt(var + EPS) * lng[:, sl] + lnb[:, sl])
    return jnp.concatenate(outs, axis=-1)


def _mixer_kernel(x_ref, s0_ref, gmix_ref, win_ref, lbl_ref, mst_ref, msk_ref, hng_ref, lng_ref, lnb_ref,
                  ws_ref, bsb_ref, wout_ref, x1_ref, sfin_ref, vn_ref,
                  st_ref, q_s, k_s, v_s, lf_s, oa_s, *, tl, nvalid, per_chunk, gchunk):
    j = pl.program_id(1)
    c = HGRN_CHUNK

    if not per_chunk:
        @pl.when(j == 0)
        def _():
            for hd in range(A_HEADS):
                st_ref[hd] = s0_ref[0, hd].T

    x = x_ref[0]
    h = _rms(x, gmix_ref[...]).astype(BF16)
    p = _dot(h, win_ref[...])
    lb = _lower_bound(lbl_ref[...])
    q, k, lf, v = _hgrn_gates(p, lb)
    if nvalid < c:
        row = lax.broadcasted_iota(jnp.int32, (tl, 1), 0)
        lf = jnp.where(row % c < nvalid, lf, 0.0)
    q_s[...] = q
    k_s[...] = k
    lf_s[...] = lf
    v_s[...] = v

    sub = lax.broadcasted_iota(jnp.int32, (c // _DIAG, _DIAG, 1), 1)

    def chunk(ci, carry):
        r0 = pl.multiple_of(ci * c, c)
        lfc = lf_s[pl.ds(r0, c), :]
        hi, lo = _split_bf16(lfc)
        mst = mst_ref[...]
        short = nvalid <= _DIAG
        nlev = 0 if short else len(_LEVELS)
        if short:
            mst = jnp.concatenate([mst[0:c], mst[4 * c:5 * c]], axis=0)
        bexp = _dot(mst, hi) + _dot(mst, lo)
        e = jnp.exp(bexp)
        qc = q_s[pl.ds(r0, c), :]
        kc = k_s[pl.ds(r0, c), :]
        vc = v_s[pl.ds(r0, c), :]
        for hd in range(A_HEADS):
            sl = slice(hd * A_DK, (hd + 1) * A_DK)
            qh, kh, vh, eh = qc[:, sl], kc[:, sl], vc[:, sl], e[:, sl]
            vb = vh.astype(BF16)
            st = s0_ref[ci, hd].T if per_chunk else st_ref[hd]
            o = jnp.zeros((c, A_DK), F32)
            if nlev:
                sc = jnp.zeros((c, c), F32)
                for li in range(nlev):
                    el = eh[(li + 1) * c:(li + 2) * c]
                    sc = sc + _dot_nt((qh * el).astype(BF16), (kh * el).astype(BF16)) * msk_ref[li]
                o = _dot(sc.astype(BF16), vb)
            o = o + _dot_nt((qh * eh[0:c]).astype(BF16), st.astype(BF16))
            nblk = 1 if short else c // _DIAG
            nrow = nblk * _DIAG
            b3 = bexp[0:nrow, sl].reshape(nblk, _DIAG, A_DK)
            q3 = qh[0:nrow].reshape(nblk, _DIAG, A_DK)
            k3 = kh[0:nrow].reshape(nblk, _DIAG, A_DK)
            v3 = vh[0:nrow].reshape(nblk, _DIAG, A_DK)
            od = jnp.zeros((nblk, _DIAG, A_DK), F32)
            for s in range(min(nvalid, _DIAG)):
                dec = jnp.exp(jnp.minimum(b3 - b3[:, s:s + 1, :], 0.0))
                w = jnp.sum(q3 * dec * k3[:, s:s + 1, :], axis=-1, keepdims=True)
                w = jnp.where(sub[0:nblk] >= s, w, 0.0)
                od = od + w * v3[:, s:s + 1, :]
            od = od.reshape(nrow, A_DK)
            if nrow < c:
                od = jnp.concatenate([od, jnp.zeros((c - nrow, A_DK), F32)], axis=0)
            oa_s[pl.ds(r0, c), sl] = o + od
            kd = (kh * eh[(nlev + 1) * c:(nlev + 2) * c]).astype(BF16)
            st_new = st * eh[c - 1:c] + _dot_tn(vb, kd)
            if per_chunk:
                sfin_ref[ci, hd] = st_new.T
            else:
                st_ref[hd] = st_new
        return carry

    lax.fori_loop(0, tl // c, chunk, 0, unroll=True)

    o_a = _head_norm_gate(oa_s[...], hng_ref[...], p[:, 3 * A_WIDTH:4 * A_WIDTH])
    ug = _gelu(p[:, 4 * A_WIDTH:4 * A_WIDTH + B_WIDTH])
    vn = _gmlp_norm(p[:, 4 * A_WIDTH + B_WIDTH:], lng_ref[...], lnb_ref[...])
    vn_ref[0] = vn
    sg = []
    for gi in range(B_GROUPS):
        sl = slice(gi * B_CH, (gi + 1) * B_CH)
        wsg = ws_ref[gi][0:gchunk, 0:gchunk]
        bsg = bsb_ref[gi][0:gchunk, :]
        rows = []
        for cc in range(tl // gchunk):
            vcc = vn[cc * gchunk:(cc + 1) * gchunk, sl].astype(BF16)
            rows.append(_dot(wsg, vcc) + bsg)
        sg.append(jnp.concatenate(rows, axis=0))
    o_b = ug * jnp.concatenate(sg, axis=-1)
    y = _dot(o_a.astype(BF16), wout_ref[0:A_WIDTH, :]) + _dot(o_b.astype(BF16), wout_ref[A_WIDTH:, :])
    x1_ref[0] = x + y

    if not per_chunk:
        @pl.when(j == pl.num_programs(1) - 1)
        def _():
            for hd in range(A_HEADS):
                sfin_ref[0, hd] = st_ref[hd].T


def _mixer_weights(norm_mix_g, w_in, lb_logits, hgrn_norm_g, ln_g, ln_b, w_s, b_s, w_out):
    mst, msk = _hgrn_consts()
    ws_bf = jnp.tril(w_s).astype(BF16)
    bsb = jnp.broadcast_to(b_s[:, :, None], (B_GROUPS, GMLP_CHUNK, B_CH))
    return (norm_mix_g.reshape(1, D_MODEL), w_in.astype(BF16), lb_logits, jnp.asarray(mst, BF16), jnp.asarray(msk, F32),
            hgrn_norm_g.reshape(1, A_WIDTH), ln_g.reshape(1, B_WIDTH), ln_b.reshape(1, B_WIDTH), ws_bf, bsb,
            w_out.astype(BF16))


def _mixer(x, s0, weights, *, tl, nvalid, per_chunk):
    nb, seq, d = x.shape
    c = HGRN_CHUNK
    spb = tl // c if per_chunk else 1
    gchunk = c if per_chunk else GMLP_CHUNK
    blk = pl.BlockSpec((1, tl, d), lambda b, j: (b, j, 0))
    sblk = pl.BlockSpec((spb, A_HEADS, A_DK, A_DK), lambda b, j: (b, 0, 0, 0))
    act = pltpu.VMEM((tl, A_WIDTH), F32)
    return pl.pallas_call(
        functools.partial(_mixer_kernel, tl=tl, nvalid=nvalid, per_chunk=per_chunk, gchunk=gchunk),
        grid=(nb, seq // tl),
        in_specs=[blk, sblk] + [_const_spec(w.shape) for w in weights],
        out_specs=[blk, sblk, pl.BlockSpec((1, tl, B_WIDTH), lambda b, j: (b, j, 0))],
        out_shape=[jax.ShapeDtypeStruct((nb, seq, d), F32),
                   jax.ShapeDtypeStruct(s0.shape, F32),
                   jax.ShapeDtypeStruct((nb, seq, B_WIDTH), F32)],
        scratch_shapes=[pltpu.VMEM((A_HEADS, A_DK, A_DK), F32), act, act, act, act, act],
        compiler_params=_params(("arbitrary", "arbitrary")),
        name="mixer",
    )(x, s0, *weights)


def _attend(q, k, v):
    s = _dot_nt(q.astype(BF16), k.astype(BF16)) * (X_HD ** -0.5)
    pexp = jnp.exp(s - jnp.max(s, axis=-1, keepdims=True))
    pr = pexp / jnp.sum(pexp, axis=-1, keepdims=True)
    return _dot(pr.astype(BF16), v.astype(BF16))


def _attn_tail(x2, gf_ref, wr_ref, br_ref, x2_ref, hn_ref, lg_ref):
    x2_ref[0] = x2
    hn = _rms(x2, gf_ref[...]).astype(BF16)
    hn_ref[0] = hn
    lg_ref[0] = _dot(hn, wr_ref[...]) + br_ref[...]


def _attn_kernel(x_ref, k_ref, v_ref, gx_ref, wq_ref, wo_ref, gf_ref, wr_ref, br_ref, x2_ref, hn_ref, lg_ref):
    x = x_ref[0]
    h = _rms(x, gx_ref[...]).astype(BF16)
    q = _dot(h, wq_ref[...])
    outs = []
    for hd in range(X_HEADS):
        sl = slice(hd * X_HD, (hd + 1) * X_HD)
        outs.append(_attend(q[:, sl], k_ref[0, :, sl], v_ref[0, :, sl]))
    o = jnp.concatenate(outs, axis=-1).astype(BF16)
    _attn_tail(x + _dot(o, wo_ref[...]), gf_ref, wr_ref, br_ref, x2_ref, hn_ref, lg_ref)


def _attn_by_head_kernel(x_ref, k_ref, v_ref, gx_ref, wq_ref, wo_ref, gf_ref, wr_ref, br_ref, x2_ref, hn_ref, lg_ref,
                         *, nseq):
    x = x_ref[0]
    rows = x.shape[0] // nseq
    h = _rms(x, gx_ref[...]).astype(BF16)
    q = _dot(h, wq_ref[...])
    per_seq = []
    for g in range(nseq):
        qg = q[g * rows:(g + 1) * rows]
        outs = [_attend(qg[:, hd * X_HD:(hd + 1) * X_HD], k_ref[g, :, hd, :], v_ref[g, :, hd, :])
                for hd in range(X_HEADS)]
        per_seq.append(jnp.concatenate(outs, axis=-1))
    o = jnp.concatenate(per_seq, axis=0).astype(BF16)
    _attn_tail(x + _dot(o, wo_ref[...]), gf_ref, wr_ref, br_ref, x2_ref, hn_ref, lg_ref)


def _attn_weights(norm_x_g, w_xq, w_xo, norm_ffn_g, w_router, b_router):
    wr = jnp.zeros((D_MODEL, LANES), BF16).at[:, :N_EXPERTS].set(w_router.astype(BF16))
    br = jnp.full((1, LANES), -1e30, F32).at[0, :N_EXPERTS].set(b_router)
    return (norm_x_g.reshape(1, D_MODEL), w_xq.astype(BF16), w_xo.astype(BF16), norm_ffn_g.reshape(1, D_MODEL), wr, br)


def _attn_out(nb, seq, d):
    return [jax.ShapeDtypeStruct((nb, seq, d), F32), jax.ShapeDtypeStruct((nb, seq, d), BF16),
            jax.ShapeDtypeStruct((nb, seq, LANES), F32)]


def _attn(x, mem_k, mem_v, weights, *, tl):
    nb, seq, d = x.shape
    blk = pl.BlockSpec((1, tl, d), lambda b, j: (b, j, 0))
    kv = pl.BlockSpec((1, MEM_LEN, d), lambda b, j: (b, 0, 0))
    return pl.pallas_call(
        _attn_kernel,
        grid=(nb, seq // tl),
        in_specs=[blk, kv, kv] + [_const_spec(w.shape) for w in weights],
        out_specs=[blk, blk, pl.BlockSpec((1, tl, LANES), lambda b, j: (b, j, 0))],
        out_shape=_attn_out(nb, seq, d),
        compiler_params=_params(("arbitrary", "arbitrary")),
        name="attn",
    )(x, mem_k, mem_v, *weights)


def _attn_by_head(x, mem_k, mem_v, weights, *, nseq):
    nb, rows, d = x.shape
    blk = pl.BlockSpec((1, rows, d), lambda b: (b, 0, 0))
    kv = pl.BlockSpec((None, nseq, MEM_LEN, X_HEADS, X_HD), lambda b: (0, b, 0, 0, 0))
    return pl.pallas_call(
        functools.partial(_attn_by_head_kernel, nseq=nseq),
        grid=(nb,),
        in_specs=[blk, kv, kv] + [_const_spec(w.shape) for w in weights],
        out_specs=[blk, blk, pl.BlockSpec((1, rows, LANES), lambda b: (b, 0, 0))],
        out_shape=_attn_out(nb, rows, d),
        compiler_params=_params(("arbitrary",)),
        name="attn_by_head",
    )(x, mem_k, mem_v, *weights)


def _gates(work):
    lane = lax.broadcasted_iota(jnp.int32, work.shape, 1).astype(F32)
    vals, hots = [], []
    for _ in range(TOP_K):
        m = jnp.max(work, axis=-1, keepdims=True)
        idx = jnp.min(jnp.where(work == m, lane, float(LANES)), axis=-1, keepdims=True)
        hot = lane == idx
        vals.append(m)
        hots.append(hot)
        work = jnp.where(hot, -jnp.inf, work)
    es = [jnp.exp(v - vals[0]) for v in vals]
    tot = es[0] + es[1] + es[2] + es[3]
    gm = jnp.zeros(work.shape, F32)
    for hot, e in zip(hots, es):
        gm = jnp.where(hot, e / tot, gm)
    return gm


MOE_TB = 1536
MOE_CAP = 224


def _moe_kernel(xn_ref, lg_ref, w1_ref, b1_ref, w2_ref, b2_ref, y_ref, g_s, c_s, gt_s, ct_s):
    e = pl.program_id(1)
    tb, cap = MOE_TB, MOE_CAP

    @pl.when(e == 0)
    def _():
        gm = _gates(lg_ref[...])
        hot = (gm > 0.0).astype(BF16)
        r = lax.broadcasted_iota(jnp.int32, (tb, tb), 0)
        cidx = lax.broadcasted_iota(jnp.int32, (tb, tb), 1)
        low = (cidx < r).astype(BF16)
        g_s[...] = gm
        c_s[...] = _dot(low, hot)
        gt = gm.T
        gt_s[...] = gt
        up = (r < cidx).astype(BF16)
        ct_s[...] = _dot((gt > 0.0).astype(BF16), up)
        y_ref[...] = jnp.zeros_like(y_ref)

    g_row = gt_s[pl.ds(e, 1), :]
    c_row = ct_s[pl.ds(e, 1), :]
    n = jnp.sum((g_row > 0.0).astype(F32)).astype(jnp.int32)
    sub_c = lax.broadcasted_iota(jnp.int32, (cap, 1), 0).astype(F32)
    lane_c = lax.broadcasted_iota(jnp.int32, (1, cap), 1).astype(F32)

    def one_pass(s, carry):
        pick = lax.broadcasted_iota(jnp.int32, (tb, LANES), 1) == e
        g_col = jnp.sum(jnp.where(pick, g_s[...], 0.0), axis=-1, keepdims=True)
        c_col = jnp.sum(jnp.where(pick, c_s[...], 0.0), axis=-1, keepdims=True)
        base = (s * cap).astype(F32)
        hit = (g_row > 0.0) & (c_row - base == sub_c)
        p = jnp.where(hit, 1.0, 0.0).astype(BF16)
        g_rows = jnp.sum(jnp.where(hit, g_row, 0.0), axis=-1, keepdims=True)
        xin = _dot(p, xn_ref[...]).astype(BF16)
        hh = _dot(xin, w1_ref[0]) + b1_ref[0]
        glu = jnp.minimum(hh[:, :D_FF], SWIGLU_LIMIT)
        lin = jnp.clip(hh[:, D_FF:], -SWIGLU_LIMIT, SWIGLU_LIMIT)
        act = glu * _sigmoid(SWIGLU_ALPHA * glu) * (lin + 1.0)
        out = (_dot(act.astype(BF16), w2_ref[0]) + b2_ref[0]) * g_rows
        hit_t = (g_col > 0.0) & (c_col - base == lane_c)
        pt = jnp.where(hit_t, 1.0, 0.0).astype(BF16)
        y_ref[...] += _dot(pt, out.astype(BF16))
        return carry

    lax.fori_loop(0, (n + cap - 1) // cap, one_pass, 0)


def _moe(xn, logits, w1_bf, b1, w2_bf, b2):
    t, d = xn.shape
    tb = MOE_TB
    return pl.pallas_call(
        _moe_kernel,
        grid=(t // tb, N_EXPERTS),
        in_specs=[pl.BlockSpec((tb, d), lambda i, e: (i, 0)),
                  pl.BlockSpec((tb, LANES), lambda i, e: (i, 0)),
                  pl.BlockSpec((1, d, 2 * D_FF), lambda i, e: (e, 0, 0)),
                  pl.BlockSpec((1, 1, 2 * D_FF), lambda i, e: (e, 0, 0)),
                  pl.BlockSpec((1, D_FF, d), lambda i, e: (e, 0, 0)),
                  pl.BlockSpec((1, 1, d), lambda i, e: (e, 0, 0))],
        out_specs=pl.BlockSpec((tb, d), lambda i, e: (i, 0)),
        out_shape=jax.ShapeDtypeStruct((t, d), F32),
        scratch_shapes=[pltpu.VMEM((tb, LANES), F32), pltpu.VMEM((tb, LANES), F32),
                        pltpu.VMEM((LANES, tb), F32), pltpu.VMEM((LANES, tb), F32)],
        compiler_params=_params(("arbitrary", "arbitrary")),
        name="moe",
    )(xn, logits, w1_bf, b1, w2_bf, b2)


def _final_kernel(x_ref, y_ref, g_ref, o_ref):
    o_ref[...] = _rms(x_ref[...] + y_ref[...], g_ref[...])


def _final(x2, y_all, row0, g, tf=512):
    t, d = x2.shape
    off = row0 // tf
    return pl.pallas_call(
        _final_kernel, grid=(t // tf,),
        in_specs=[pl.BlockSpec((tf, d), lambda i: (i, 0)), pl.BlockSpec((tf, d), lambda i: (i + off, 0)),
                  _const_spec((1, d))],
        out_specs=pl.BlockSpec((tf, d), lambda i: (i, 0)),
        out_shape=jax.ShapeDtypeStruct((t, d), F32),
        compiler_params=_params(("arbitrary",)), name="final",
    )(x2, y_all, g)


SAMPLE_PAD = HGRN_CHUNK
STEP_ROWS = 256
ATTN_SEQS = 2


def kernel(x_prompt, x_sample, mem_prompt, state_hgrn, cache_mem_k, cache_mem_v, norm_mix_g, w_in, hgrn_lb_logits,
           hgrn_norm_g, gmlp_ln_g, gmlp_ln_b, gmlp_w_s, gmlp_b_s, w_out, norm_x_g, norm_mem_g, w_xq, w_xk, w_xv, w_xo,
           norm_ffn_g, w_router, b_router, w1, b1, w2, b2, norm_final_g):
    assert w_in.shape[0] == 1, "one layer"
    nbp, seq, d = x_prompt.shape
    nbs, dseq, _ = x_sample.shape
    mw = _mixer_weights(norm_mix_g[0], w_in[0], hgrn_lb_logits, hgrn_norm_g[0], gmlp_ln_g[0], gmlp_ln_b[0],
                        gmlp_w_s[0], gmlp_b_s[0], w_out[0])
    aw = _attn_weights(norm_x_g[0], w_xq[0], w_xo[0], norm_ffn_g[0], w_router[0], b_router[0])

    mk, mv, mkb, mvb = _memkv(mem_prompt, norm_mem_g[0].reshape(1, d), w_xk[0].astype(BF16), w_xv[0].astype(BF16))
    s0p = jnp.zeros((nbp, A_HEADS, A_DK, A_DK), F32)
    x1p, s_p, _ = _mixer(x_prompt, s0p, mw, tl=STEP_ROWS, nvalid=HGRN_CHUNK, per_chunk=False)
    x2p, hnp, lgp = _attn(x1p, mkb, mvb, aw, tl=STEP_ROWS)

    spb = STEP_ROWS // SAMPLE_PAD
    xs = jnp.pad(x_sample, ((0, 0), (0, SAMPLE_PAD - dseq), (0, 0))).reshape(nbs // spb, STEP_ROWS, d)
    x1s, s_s, vns = _mixer(xs, state_hgrn[0], mw, tl=STEP_ROWS, nvalid=dseq, per_chunk=True)
    x2s, hns, lgs = _attn_by_head(x1s.reshape(nbs // ATTN_SEQS, ATTN_SEQS * SAMPLE_PAD, d), cache_mem_k, cache_mem_v,
                                  aw, nseq=ATTN_SEQS)

    def real(a):
        return a.reshape(nbs, SAMPLE_PAD, a.shape[-1])[:, :dseq].reshape(nbs * dseq, a.shape[-1])

    tp = nbp * seq

    xn = jnp.concatenate([hnp.reshape(tp, d), real(hns)], axis=0)
    logits = jnp.concatenate([lgp.reshape(tp, LANES), real(lgs)], axis=0)
    y_all = _moe(xn, logits, w1[0].astype(BF16), b1[0].reshape(N_EXPERTS, 1, 2 * D_FF), w2[0].astype(BF16),
                 b2[0].reshape(N_EXPERTS, 1, d))
    gfin = norm_final_g.reshape(1, d)
    y_prompt = _final(x2p.reshape(tp, d), y_all, 0, gfin).reshape(nbp, seq, d)
    y_sample = _final(real(x2s), y_all, tp, gfin).reshape(nbs, dseq, d)

    vn_s = real(vns).reshape(1, nbs, dseq, B_GROUPS, B_CH)
    return (y_prompt, y_sample, s_p[None], mk.reshape(1, nbp, MEM_LEN, X_HEADS, X_HD),
            mv.reshape(1, nbp, MEM_LEN, X_HEADS, X_HD), s_s[None], vn_s)
```
